```python
import math
import jax, jax.numpy as jnp
from jax import lax
import numpy as np

D_MODEL = 1024
BATCH = 4
SEQ = 8192
DEPTH = 2

CTX_LEN = 256
GRID_W = 64
HEAD_DIM = 64
LRU_WIDTH = 512
LRU_BLOCKS = 8
LRU_BLOCK = LRU_WIDTH // LRU_BLOCKS
CONV_W = 4
CONV_LEFT = 2
LRU_C = 8.0
GQA_HEADS = 8
GQA_KV_HEADS = 2
GQA_GROUP = GQA_HEADS // GQA_KV_HEADS
DIFF_HEADS = 4
DIFF_V_DIM = 2 * HEAD_DIM
BRANCH_W = 512
N_BRANCH = 3
D_FF = 4 * D_MODEL
Q_BLOCK = 128
ROPE_THETA = 10000.0
EPS = 1e-6
IN_SECTIONS = (LRU_WIDTH, LRU_WIDTH, GQA_HEADS * HEAD_DIM, GQA_KV_HEADS * HEAD_DIM, GQA_KV_HEADS * HEAD_DIM, DIFF_HEADS * 2 * HEAD_DIM, DIFF_HEADS * 2 * HEAD_DIM, DIFF_HEADS * DIFF_V_DIM, N_BRANCH * D_MODEL)
N_IN = 2 * LRU_WIDTH + (GQA_HEADS + 2 * GQA_KV_HEADS) * HEAD_DIM + DIFF_HEADS * (4 * HEAD_DIM + DIFF_V_DIM) + N_BRANCH * D_MODEL

kernel_name = 'hybrid_dit_rglru_gqa_diffattn'


def split_cols(z):
    idx, acc = [], 0
    for w in IN_SECTIONS[:-1]:
        acc += w
        idx.append(acc)
    return jnp.split(z, idx, axis=-1)


def rmsnorm(x, g):
    xf = x.astype(jnp.float32)
    y = xf * lax.rsqrt(jnp.mean(xf * xf, axis=-1, keepdims=True) + EPS)
    return (y * g.astype(jnp.float32)).astype(x.dtype)


def axial_rope_tables(rows):
    row = jnp.broadcast_to(jnp.arange(rows, dtype=jnp.float32)[:, None], (rows, GRID_W)).reshape(-1)
    col = jnp.broadcast_to(jnp.arange(GRID_W, dtype=jnp.float32)[None, :], (rows, GRID_W)).reshape(-1)
    n_freq = HEAD_DIM // 4
    inv = ROPE_THETA ** (-jnp.arange(n_freq, dtype=jnp.float32) * 2.0 / (HEAD_DIM // 2))
    ang = jnp.concatenate([row[:, None] * inv, col[:, None] * inv], axis=-1)
    return jnp.cos(ang), jnp.sin(ang)


def apply_axial_rope(x, cos, sin):
    n_freq = HEAD_DIM // 4
    xs = x.reshape(x.shape[:-1] + (2, 2, n_freq))
    x1 = xs[..., 0, :]
    x2 = xs[..., 1, :]
    cs = cos.reshape(cos.shape[0], 1, 2, n_freq).astype(x.dtype)
    sn = sin.reshape(sin.shape[0], 1, 2, n_freq).astype(x.dtype)
    out = jnp.stack([x1 * cs - x2 * sn, x1 * sn + x2 * cs], axis=-2)
    return out.reshape(x.shape)


def centred_dwconv(u, w, bias):
    t = u.shape[1]
    up = jnp.pad(u, ((0, 0), (CONV_LEFT, CONV_W - 1 - CONV_LEFT), (0, 0)))
    y = up[:, 0:t] * w[0]
    for j in range(1, CONV_W):
        y = y + up[:, j:j + t] * w[j]
    return y + bias


def block_diag_linear(u, w, bias):
    us = u.reshape(u.shape[:-1] + (LRU_BLOCKS, LRU_BLOCK))
    return jnp.einsum('btnc,ncd->btnd', us, w).reshape(u.shape) + bias


def rglru_coeffs(u, w_a, b_a, w_x, b_x, lam):
    uf = u.astype(jnp.float32)
    r = jax.nn.sigmoid(block_diag_linear(uf, w_a.astype(jnp.float32), b_a.astype(jnp.float32)))
    i = jax.nn.sigmoid(block_diag_linear(uf, w_x.astype(jnp.float32), b_x.astype(jnp.float32)))
    log_a = -LRU_C * r * jax.nn.softplus(-lam.astype(jnp.float32))
    a = jnp.exp(log_a)
    b = jnp.sqrt(-jnp.expm1(2.0 * log_a)) * (i * uf)
    return a, b


def _scan_combine(left, right):
    a_l, b_l = left
    a_r, b_r = right
    return a_l * a_r, a_r * b_l + b_r


def linear_scan(a, b, h0, reverse):
    if reverse:
        a = jnp.flip(a, axis=1)
        b = jnp.flip(b, axis=1)
    b = b.at[:, 0].add(a[:, 0] * h0)
    _, h = lax.associative_scan(_scan_combine, (a, b), axis=1)
    if reverse:
        h = jnp.flip(h, axis=1)
    return h


def gqa_attend(q, k, v):
    s = jnp.einsum('bqgrd,bkgd->bgrqk', q, k).astype(jnp.float32) * (HEAD_DIM ** -0.5)
    p = jax.nn.softmax(s, axis=-1).astype(v.dtype)
    return jnp.einsum('bgrqk,bkgd->bqgrd', p, v)


def diff_attend(q1, q2, k1, k2, v, lam):
    scale = HEAD_DIM ** -0.5
    p1 = jax.nn.softmax(jnp.einsum('bqhd,bkhd->bhqk', q1, k1).astype(jnp.float32) * scale, axis=-1)
    p2 = jax.nn.softmax(jnp.einsum('bqhd,bkhd->bhqk', q2, k2).astype(jnp.float32) * scale, axis=-1)
    p = (p1 - lam * p2).astype(v.dtype)
    return jnp.einsum('bhqk,bkhd->bqhd', p, v)


def blocked_queries(fn, qs):
    bsz, t = qs[0].shape[:2]
    nb = t // Q_BLOCK
    blocks = tuple(jnp.moveaxis(a.reshape((bsz, nb, Q_BLOCK) + a.shape[2:]), 1, 0) for a in qs)
    out = lax.map(lambda blk: fn(*blk), blocks)
    out = jnp.moveaxis(out, 0, 1)
    return out.reshape((bsz, t) + out.shape[3:])


def sq_relu_mlp(h, w_up, w_down):
    return jnp.square(jax.nn.relu(h @ w_up)) @ w_down


def token_mixer(hl, hc, p, lam_init, cos, sin, ctx_out):
    bsz, s, _ = hl.shape
    n_ctx = hc.shape[1]
    zl = split_cols(hl @ p['w_in'])
    zc = split_cols(hc @ p['w_in'])

    def heads(z, n):
        return z.reshape(z.shape[:2] + (n, HEAD_DIM))

    ul = centred_dwconv(zl[0], p['conv_w'], p['conv_b'])
    uc = centred_dwconv(zc[0], p['conv_w'], p['conv_b'])
    hs_l, hs_c = [], []
    for d, rev in ((0, False), (1, True)):
        gp = (p['w_rg'][d], p['b_rg'][d], p['w_ig'][d], p['b_ig'][d], p['lru_lambda'][d])
        a_c, b_c = rglru_coeffs(uc, *gp)
        h_c = linear_scan(a_c, b_c, jnp.zeros((bsz, LRU_WIDTH), jnp.float32), rev)
        h0 = h_c[:, 0] if rev else h_c[:, -1]
        a_l, b_l = rglru_coeffs(ul, *gp)
        hs_l.append(linear_scan(a_l, b_l, h0, rev))
        hs_c.append(h_c)
    y_rec_l = ((hs_l[0] + hs_l[1]) * jax.nn.gelu(zl[1].astype(jnp.float32))).astype(hl.dtype)

    ql = apply_axial_rope(rmsnorm(heads(zl[2], GQA_HEADS), p['q_norm_g']), cos, sin)
    kl = apply_axial_rope(rmsnorm(heads(zl[3], GQA_KV_HEADS), p['k_norm_g']), cos, sin)
    vl = heads(zl[4], GQA_KV_HEADS)
    qc = rmsnorm(heads(zc[2], GQA_HEADS), p['q_norm_g'])
    kc = rmsnorm(heads(zc[3], GQA_KV_HEADS), p['k_norm_g'])
    vc = heads(zc[4], GQA_KV_HEADS)
    k_all = jnp.concatenate([kc, kl], axis=1)
    v_all = jnp.concatenate([vc, vl], axis=1)

    def gqa_block(qb):
        qg = qb.reshape(qb.shape[:2] + (GQA_KV_HEADS, GQA_GROUP, HEAD_DIM))
        return gqa_attend(qg, k_all, v_all)

    y_gqa_l = blocked_queries(gqa_block, (ql,)).reshape(bsz, s, GQA_HEADS * HEAD_DIM)

    lam = (jnp.exp(jnp.sum(p['lambda_q1'].astype(jnp.float32) * p['lambda_k1'].astype(jnp.float32)))
           - jnp.exp(jnp.sum(p['lambda_q2'].astype(jnp.float32) * p['lambda_k2'].astype(jnp.float32)))
           + lam_init)
    dql = apply_axial_rope(heads(zl[5], 2 * DIFF_HEADS), cos, sin).reshape(bsz, s, DIFF_HEADS, 2, HEAD_DIM)
    dkl = apply_axial_rope(heads(zl[6], 2 * DIFF_HEADS), cos, sin).reshape(bsz, s, DIFF_HEADS, 2, HEAD_DIM)
    dvl = zl[7].reshape(bsz, s, DIFF_HEADS, DIFF_V_DIM)
    dqc = zc[5].reshape(bsz, n_ctx, DIFF_HEADS, 2, HEAD_DIM)
    dkc = zc[6].reshape(bsz, n_ctx, DIFF_HEADS, 2, HEAD_DIM)
    dvc = zc[7].reshape(bsz, n_ctx, DIFF_HEADS, DIFF_V_DIM)
    dk_all = jnp.concatenate([dkc, dkl], axis=1)
    k1_all = dk_all[:, :, :, 0]
    k2_all = dk_all[:, :, :, 1]
    dv_all = jnp.concatenate([dvc, dvl], axis=1)

    def diff_block(q1b, q2b):
        return diff_attend(q1b, q2b, k1_all, k2_all, dv_all, lam)

    o_l = blocked_queries(diff_block, (dql[:, :, :, 0], dql[:, :, :, 1]))
    y_diff_l = (rmsnorm(o_l, p['subln_g']) * (1.0 - lam_init)).reshape(bsz, s, DIFF_HEADS * DIFF_V_DIM)

    def merge(ys, zg):
        g = jax.nn.sigmoid(zg + p['b_gate']).reshape(zg.shape[:2] + (N_BRANCH, D_MODEL))
        m = g[:, :, 0] * (ys[0] @ p['w_branch'][0])
        for n in range(1, N_BRANCH):
            m = m + g[:, :, n] * (ys[n] @ p['w_branch'][n])
        return m @ p['w_out']

    out_l = merge((y_rec_l, y_gqa_l, y_diff_l), zl[8])
    if not ctx_out:
        return out_l, None

    y_rec_c = ((hs_c[0] + hs_c[1]) * jax.nn.gelu(zc[1].astype(jnp.float32))).astype(hc.dtype)
    qcg = qc.reshape(bsz, n_ctx, GQA_KV_HEADS, GQA_GROUP, HEAD_DIM)
    y_gqa_c = gqa_attend(qcg, kc, vc).reshape(bsz, n_ctx, GQA_HEADS * HEAD_DIM)
    o_c = diff_attend(dqc[:, :, :, 0], dqc[:, :, :, 1], dkc[:, :, :, 0], dkc[:, :, :, 1], dvc, lam)
    y_diff_c = (rmsnorm(o_c, p['subln_g']) * (1.0 - lam_init)).reshape(bsz, n_ctx, DIFF_HEADS * DIFF_V_DIM)
    out_c = merge((y_rec_c, y_gqa_c, y_diff_c), zc[8])
    return out_l, out_c


def setup_inputs(seed: int = 0) -> dict:
    key = jax.random.key(seed)
    ks = jax.random.split(key, 32)
    f32 = jnp.float32

    def nrm(k, shape, scale):
        return jax.random.normal(k, shape, f32) * scale

    u = jax.random.uniform(ks[15], (DEPTH, 2, LRU_WIDTH), f32, 0.9, 0.999)
    a = u ** (1.0 / LRU_C)
    lru_lambda = jnp.log(a) - jnp.log1p(-a)
    return {
        'x': nrm(ks[0], (BATCH, SEQ, D_MODEL), 1.0),
        'c': nrm(ks[1], (BATCH, D_MODEL), 1.0),
        'ctx': nrm(ks[2], (BATCH, CTX_LEN, D_MODEL), 1.0),
        'c_ctx': nrm(ks[3], (D_MODEL,), 1.0),
        'w_mod': nrm(ks[4], (DEPTH, D_MODEL, 6 * D_MODEL), 0.5 * D_MODEL ** -0.5),
        'b_mod': nrm(ks[5], (DEPTH, 6 * D_MODEL), 0.02),
        'norm1_g': 1.0 + nrm(ks[6], (DEPTH, D_MODEL), 0.1),
        'w_in': nrm(ks[7], (DEPTH, D_MODEL, N_IN), D_MODEL ** -0.5),
        'b_gate': nrm(ks[8], (DEPTH, N_BRANCH * D_MODEL), 0.1),
        'conv_w': nrm(ks[9], (DEPTH, CONV_W, LRU_WIDTH), CONV_W ** -0.5),
        'conv_b': nrm(ks[10], (DEPTH, LRU_WIDTH), 0.02),
        'w_rg': nrm(ks[11], (DEPTH, 2, LRU_BLOCKS, LRU_BLOCK, LRU_BLOCK), LRU_BLOCK ** -0.5),
        'b_rg': nrm(ks[12], (DEPTH, 2, LRU_WIDTH), 0.1),
        'w_ig': nrm(ks[13], (DEPTH, 2, LRU_BLOCKS, LRU_BLOCK, LRU_BLOCK), LRU_BLOCK ** -0.5),
        'b_ig': nrm(ks[14], (DEPTH, 2, LRU_WIDTH), 0.1),
        'lru_lambda': lru_lambda,
        'q_norm_g': 1.0 + nrm(ks[16], (DEPTH, HEAD_DIM), 0.1),
        'k_norm_g': 1.0 + nrm(ks[17], (DEPTH, HEAD_DIM), 0.1),
        'lambda_q1': nrm(ks[18], (DEPTH, HEAD_DIM), 0.1),
        'lambda_k1': nrm(ks[19], (DEPTH, HEAD_DIM), 0.1),
        'lambda_q2': nrm(ks[20], (DEPTH, HEAD_DIM), 0.1),
        'lambda_k2': nrm(ks[21], (DEPTH, HEAD_DIM), 0.1),
        'subln_g': 1.0 + nrm(ks[22], (DEPTH, DIFF_V_DIM), 0.1),
        'w_branch': nrm(ks[23], (DEPTH, N_BRANCH, BRANCH_W, D_MODEL), BRANCH_W ** -0.5),
        'w_out': nrm(ks[24], (DEPTH, D_MODEL, D_MODEL), D_MODEL ** -0.5),
        'norm2_g': 1.0 + nrm(ks[25], (DEPTH, D_MODEL), 0.1),
        'w_up': nrm(ks[26], (DEPTH, D_MODEL, D_FF), D_MODEL ** -0.5),
        'w_down': nrm(ks[27], (DEPTH, D_FF, D_MODEL), D_FF ** -0.5),
        'final_g': 1.0 + nrm(ks[28], (D_MODEL,), 0.1),
    }


def reference(x, c, ctx, c_ctx, w_mod, b_mod, norm1_g, w_in, b_gate, conv_w, conv_b, w_rg, b_rg, w_ig, b_ig, lru_lambda, q_norm_g, k_norm_g, lambda_q1, lambda_k1, lambda_q2, lambda_k2, subln_g, w_branch, w_out, norm2_g, w_up, w_down, final_g):
    rows = x.shape[1] // GRID_W
    cos, sin = axial_rope_tables(rows)
    sc = jax.nn.silu(c)
    scc = jax.nn.silu(c_ctx)
    xc = ctx
    for l in range(DEPTH):
        last = l == DEPTH - 1
        lam_init = 0.8 - 0.6 * math.exp(-0.3 * l)
        mod_l = [m[:, None, :] for m in jnp.split(sc @ w_mod[l] + b_mod[l], 6, axis=-1)]
        mod_c = [m[None, None, :] for m in jnp.split(scc @ w_mod[l] + b_mod[l], 6, axis=-1)]
        p = {
            'w_in': w_in[l], 'b_gate': b_gate[l], 'conv_w': conv_w[l], 'conv_b': conv_b[l],
            'w_rg': w_rg[l], 'b_rg': b_rg[l], 'w_ig': w_ig[l], 'b_ig': b_ig[l],
            'lru_lambda': lru_lambda[l], 'q_norm_g': q_norm_g[l], 'k_norm_g': k_norm_g[l],
            'lambda_q1': lambda_q1[l], 'lambda_k1': lambda_k1[l],
            'lambda_q2': lambda_q2[l], 'lambda_k2': lambda_k2[l],
            'subln_g': subln_g[l], 'w_branch': w_branch[l], 'w_out': w_out[l],
        }
        hl = rmsnorm(x, norm1_g[l]) * (1.0 + mod_l[1]) + mod_l[0]
        hc = rmsnorm(xc, norm1_g[l]) * (1.0 + mod_c[1]) + mod_c[0]
        yl, yc = token_mixer(hl, hc, p, lam_init, cos, sin, not last)
        x = x + mod_l[2] * yl
        x = x + mod_l[5] * sq_relu_mlp(rmsnorm(x, norm2_g[l]) * (1.0 + mod_l[4]) + mod_l[3], w_up[l], w_down[l])
        if not last:
            xc = xc + mod_c[2] * yc
            xc = xc + mod_c[5] * sq_relu_mlp(rmsnorm(xc, norm2_g[l]) * (1.0 + mod_c[4]) + mod_c[3], w_up[l], w_down[l])
    return rmsnorm(x, final_g)
```

```python
import functools
import math

import jax
import jax.numpy as jnp
from jax import lax
from jax.experimental import pallas as pl
from jax.experimental.pallas import tpu as pltpu

GRID_W = 64
HEAD_DIM = 64
LRU_WIDTH = 512
LRU_BLOCKS = 8
CONV_W = 4
CONV_LEFT = 2
LRU_C = 8.0
GQA_HEADS = 8
GQA_KV_HEADS = 2
GQA_GROUP = GQA_HEADS // GQA_KV_HEADS
DIFF_HEADS = 4
DIFF_V_DIM = 2 * HEAD_DIM
N_BRANCH = 3
ROPE_THETA = 10000.0
EPS = 1e-6
SCALE = HEAD_DIM ** -0.5

TM = 256
HALO = 8
LANES = 128
MOD_ROWS = 8
VMEM_LIMIT = 56 * 1024 * 1024

N_PROJ = 2 * LRU_WIDTH + (GQA_HEADS + 2 * GQA_KV_HEADS) * HEAD_DIM + DIFF_HEADS * (4 * HEAD_DIM + DIFF_V_DIM)

_BF = jnp.bfloat16
_F32 = jnp.float32


def _params(sem):
    return pltpu.CompilerParams(dimension_semantics=sem, vmem_limit_bytes=VMEM_LIMIT)


def _resident(shape, index_map):
    return pl.BlockSpec(shape, index_map, pipeline_mode=pl.Buffered(1))


def _split_bf16(a):
    hi = a.astype(_BF)
    lo = (a - hi.astype(_F32)).astype(_BF)
    return hi, lo


def _dot(a, b):
    return jnp.dot(a, b, preferred_element_type=_F32)


def _mod_kernel(c_ref, w_ref, b_ref, o_ref):
    c = c_ref[...]
    s = c * jax.nn.sigmoid(c)
    s_hi, s_lo = _split_bf16(s)
    w_hi, w_lo = _split_bf16(w_ref[...])
    o_ref[...] = _dot(s_hi, w_hi) + (_dot(s_lo, w_hi) + _dot(s_hi, w_lo)) + b_ref[...]


def _modulation(cin, w_mod, b_mod):
    depth, d, n = w_mod.shape
    tn = 1536
    return pl.pallas_call(
        _mod_kernel,
        grid=(depth, n // tn),
        in_specs=[
            pl.BlockSpec((MOD_ROWS, d), lambda l, j: (0, 0)),
            pl.BlockSpec((None, d, tn), lambda l, j: (l, 0, j)),
            pl.BlockSpec((None, 1, tn), lambda l, j: (l, 0, j)),
        ],
        out_specs=pl.BlockSpec((None, MOD_ROWS, tn), lambda l, j: (l, 0, j)),
        out_shape=jax.ShapeDtypeStruct((depth, MOD_ROWS, n), _F32),
        compiler_params=_params(("arbitrary", "arbitrary")),
        name="modulation",
    )(cin, w_mod, b_mod.reshape(depth, 1, n))


def _rms_mod(x, g, scale, shift):
    y = x * lax.rsqrt(jnp.mean(x * x, axis=-1, keepdims=True) + EPS) * g
    return y * (1.0 + scale) + shift


def _head_sumsq(c, bd):
    hi, lo = _split_bf16(c * c)
    return _dot(hi, bd) + _dot(lo, bd)


def _swap16(y):
    lane = lax.broadcasted_iota(jnp.int32, y.shape, 1)
    return jnp.where(lane % 32 < 16, pltpu.roll(y, LANES - 16, 1), pltpu.roll(y, 16, 1))


def _rope(y, cos, sin):
    return y * cos + _swap16(y) * sin


def _inproj_kernel(x_ref, mod_ref, g1_ref, w_ref, gq_ref, gk_ref, cos_ref, sin_ref, bd_ref,
                   z0_ref, gz1_ref, q_ref, k_ref, v_ref, dq_ref, dk_ref, dv_ref):
    d = x_ref.shape[-1]
    shift = mod_ref[0, :, 0:d]
    scale = mod_ref[0, :, d:2 * d]
    hn = _rms_mod(x_ref[0], g1_ref[...], scale, shift).astype(_BF)
    cos = cos_ref[...]
    sin = sin_ref[...]
    bd = bd_ref[...]

    def proj(a, b):
        return _dot(hn, w_ref[:, a:b])

    c0 = 0
    z0_ref[0] = proj(c0, c0 + LRU_WIDTH)
    c0 += LRU_WIDTH
    gz1_ref[0] = jax.nn.gelu(proj(c0, c0 + LRU_WIDTH))
    c0 += LRU_WIDTH
    for j in range(GQA_HEADS * HEAD_DIM // LANES):
        c = proj(c0 + j * LANES, c0 + (j + 1) * LANES)
        y = c * lax.rsqrt(_head_sumsq(c, bd) * (1.0 / HEAD_DIM) + EPS) * gq_ref[...]
        q_ref[0, :, j * LANES:(j + 1) * LANES] = (_rope(y, cos, sin) * SCALE).astype(_BF)
    c0 += GQA_HEADS * HEAD_DIM
    c = proj(c0, c0 + LANES)
    y = c * lax.rsqrt(_head_sumsq(c, bd) * (1.0 / HEAD_DIM) + EPS) * gk_ref[...]
    k_ref[0] = _rope(y, cos, sin).astype(_BF)
    c0 += GQA_KV_HEADS * HEAD_DIM
    v_ref[0] = proj(c0, c0 + LANES).astype(_BF)
    c0 += GQA_KV_HEADS * HEAD_DIM
    for j in range(DIFF_HEADS * 2 * HEAD_DIM // LANES):
        c = proj(c0 + j * LANES, c0 + (j + 1) * LANES)
        dq_ref[0, :, j * LANES:(j + 1) * LANES] = (_rope(c, cos, sin) * SCALE).astype(_BF)
    c0 += DIFF_HEADS * 2 * HEAD_DIM
    for j in range(DIFF_HEADS * 2 * HEAD_DIM // LANES):
        c = proj(c0 + j * LANES, c0 + (j + 1) * LANES)
        dk_ref[0, :, j * LANES:(j + 1) * LANES] = _rope(c, cos, sin).astype(_BF)
    c0 += DIFF_HEADS * 2 * HEAD_DIM
    dv_ref[0] = proj(c0, c0 + DIFF_HEADS * DIFF_V_DIM).astype(_BF)


def _mod_row(nct):
    def row(b, t, nb):
        return jnp.where(t < nct, nb, b)
    return row


def _inproj(xa, mod3, g1, w_proj, gq, gk, cos, sin, bd, nct):
    bsz, t, d = xa.shape
    nt = t // TM
    row = _mod_row(nct)
    tok = lambda w: pl.BlockSpec((1, TM, w), lambda b, i: (b, i, 0))
    const = lambda shape: pl.BlockSpec(shape, lambda b, i: (0,) * len(shape))
    outs = [(LRU_WIDTH, _F32), (LRU_WIDTH, _F32), (GQA_HEADS * HEAD_DIM, _BF), (GQA_KV_HEADS * HEAD_DIM, _BF),
            (GQA_KV_HEADS * HEAD_DIM, _BF), (DIFF_HEADS * 2 * HEAD_DIM, _BF), (DIFF_HEADS * 2 * HEAD_DIM, _BF),
            (DIFF_HEADS * DIFF_V_DIM, _BF)]
    return pl.pallas_call(
        _inproj_kernel,
        grid=(bsz, nt),
        in_specs=[
            tok(d),
            pl.BlockSpec((1, 1, 2 * d), lambda b, i: (row(b, i, bsz), 0, 0)),
            const((1, d)),
            _resident((d, N_PROJ), lambda b, i: (0, 0)),
            const((1, LANES)), const((1, LANES)),
            pl.BlockSpec((TM, LANES), lambda b, i: (i, 0)),
            pl.BlockSpec((TM, LANES), lambda b, i: (i, 0)),
            const((LANES, LANES)),
        ],
        out_specs=[tok(w) for w, _ in outs],
        out_shape=[jax.ShapeDtypeStruct((bsz, t, w), dt) for w, dt in outs],
        compiler_params=_params(("arbitrary", "arbitrary")),
        name="inproj",
    )(xa, mod3, g1, w_proj, gq, gk, cos, sin, bd)


def _lru_kernel(*refs, nct, nt, reverse):
    if reverse:
        (z_ref, zp_ref, zn_ref, cw_ref, cb_ref, wr_ref, br_ref, wi_ref, bi_ref, lam_ref, hf_ref, gz1_ref,
         o_ref, ext_ref, a_ref, b_ref, h_ref) = refs
    else:
        (z_ref, zp_ref, zn_ref, cw_ref, cb_ref, wr_ref, br_ref, wi_ref, bi_ref, lam_ref,
         o_ref, ext_ref, a_ref, b_ref, h_ref) = refs
    i = pl.program_id(1)
    tile = _lru_tile(i, nct, nt, reverse)

    @pl.when(i == 0)
    def _():
        h_ref[...] = jnp.zeros_like(h_ref)

    seg_first = jnp.logical_or(tile == 0, tile == nct)
    seg_last = jnp.logical_or(tile == nct - 1, tile == nt - 1)
    u = z_ref[0]
    ext_ref[0:HALO, :] = jnp.where(seg_first, 0.0, zp_ref[0])
    ext_ref[HALO:HALO + TM, :] = u
    ext_ref[HALO + TM:, :] = jnp.where(seg_last, 0.0, zn_ref[0])
    y = cb_ref[...] + cw_ref[CONV_LEFT:CONV_LEFT + 1, :] * u
    for j in range(CONV_W):
        if j != CONV_LEFT:
            y = y + cw_ref[j:j + 1, :] * ext_ref[pl.ds(HALO + j - CONV_LEFT, TM), :]

    yb = y.astype(_BF)
    r = jax.nn.sigmoid(_dot(yb, wr_ref[...]) + br_ref[...])
    ig = jax.nn.sigmoid(_dot(yb, wi_ref[...]) + bi_ref[...])
    log_a = (-LRU_C) * r * jax.nn.softplus(-lam_ref[...])
    a = jnp.exp(log_a)
    a_ref[...] = a
    b_ref[...] = jnp.sqrt(-jnp.tanh(log_a) * (a * a + 1.0)) * (ig * y)

    ng = TM // 8
    row = lax.broadcasted_iota(jnp.int32, (8, LRU_WIDTH), 0)

    def body(g, h):
        r0 = pl.multiple_of((ng - 1 - g if reverse else g) * 8, 8)
        aa = a_ref[pl.ds(r0, 8), :]
        bb = b_ref[pl.ds(r0, 8), :]
        for s in (1, 2, 4):
            sh = 8 - s if reverse else s
            m = (row < 8 - s) if reverse else (row >= s)
            bb = jnp.where(m, aa * pltpu.roll(bb, sh, 0) + bb, bb)
            aa = jnp.where(m, aa * pltpu.roll(aa, sh, 0), aa)
        hh = aa * h + bb
        if reverse:
            hsum = hf_ref[0, pl.ds(r0, 8), :] + hh
            o_ref[0, pl.ds(r0, 8), :] = (hsum * gz1_ref[0, pl.ds(r0, 8), :]).astype(o_ref.dtype)
            return hh[0:1, :]
        o_ref[0, pl.ds(r0, 8), :] = hh
        return hh[7:8, :]

    h_ref[0:1, :] = lax.fori_loop(0, ng, body, h_ref[0:1, :])


def _lru_tile(i, nct, nt, reverse):
    if not reverse:
        return i
    return jnp.where(i < nct, nct - 1 - i, nt - 1 - (i - nct))


def _lru_sweep(z0, conv_w, conv_b, w_r, b_r, w_i, b_i, lam, nct, reverse, hf=None, gz1=None):
    bsz, t, w = z0.shape
    nt = t // TM
    hb = TM // HALO
    tile = lambda i: _lru_tile(i, nct, nt, reverse)
    tok = pl.BlockSpec((1, TM, w), lambda b, i: (b, tile(i), 0))
    const = lambda shape: pl.BlockSpec(shape, lambda b, i: (0,) * len(shape))
    in_specs = [
        tok,
        pl.BlockSpec((1, HALO, w), lambda b, i: (b, jnp.maximum(tile(i) * hb - 1, 0), 0)),
        pl.BlockSpec((1, HALO, w), lambda b, i: (b, jnp.minimum((tile(i) + 1) * hb, nt * hb - 1), 0)),
        const((CONV_W, w)), const((1, w)), const((w, w)), const((1, w)), const((w, w)), const((1, w)), const((1, w)),
    ]
    args = [z0, z0, z0, conv_w, conv_b, w_r, b_r, w_i, b_i, lam]
    if reverse:
        in_specs += [tok, tok]
        args += [hf, gz1]
    return pl.pallas_call(
        functools.partial(_lru_kernel, nct=nct, nt=nt, reverse=reverse),
        grid=(bsz, nt),
        in_specs=in_specs,
        out_specs=tok,
        out_shape=jax.ShapeDtypeStruct((bsz, t, w), _BF if reverse else _F32),
        scratch_shapes=[pltpu.VMEM((TM + 2 * HALO, w), _F32), pltpu.VMEM((TM, w), _F32),
                        pltpu.VMEM((TM, w), _F32), pltpu.VMEM((8, w), _F32)],
        compiler_params=_params(("arbitrary", "arbitrary")),
        name="lru_bwd" if reverse else "lru_fwd",
    )(*args)


def _softmax_step(s, m, l):
    m_new = jnp.maximum(m, jnp.max(s, axis=-1, keepdims=True))
    alpha = jnp.exp(m - m_new)
    p = jnp.exp(s - m_new)
    return p, m_new, alpha, alpha * l + jnp.sum(p, axis=-1, keepdims=True)


def _gqa_kernel(q_ref, kt_ref, v_ref, o_ref, *, nct, tk):
    tq = q_ref.shape[3]
    rows = GQA_GROUP * tq
    q = q_ref[0, 0].reshape(rows, HEAD_DIM)
    n_ctx = nct * TM
    n_lat = kt_ref.shape[-1] - n_ctx

    def step(carry, start, size):
        m, l, acc = carry
        s = _dot(q, kt_ref[0, 0, :, pl.ds(start, size)])
        p, m, alpha, l = _softmax_step(s, m, l)
        acc = alpha * acc + _dot(p.astype(_BF), v_ref[0, 0, pl.ds(start, size), :])
        return m, l, acc

    init = (jnp.full((rows, 1), -jnp.inf, _F32), jnp.zeros((rows, 1), _F32), jnp.zeros((rows, HEAD_DIM), _F32))
    carry = step(init, 0, n_ctx)
    n_steps = jnp.where(pl.program_id(2) >= nct * (TM // tq), n_lat // tk, 0)
    carry = lax.fori_loop(
        0, n_steps, lambda j, c: step(c, pl.multiple_of(n_ctx + j * tk, math.gcd(n_ctx, tk)), tk), carry)
    m, l, acc = carry
    o_ref[0, 0] = (acc / l).reshape(GQA_GROUP, tq, HEAD_DIM).astype(o_ref.dtype)


def _gqa_attention(qh, kt, vh, nct, tq, tk):
    bsz, g, r, t, hd = qh.shape
    return pl.pallas_call(
        functools.partial(_gqa_kernel, nct=nct, tk=tk),
        grid=(bsz, g, t // tq),
        in_specs=[
            pl.BlockSpec((1, 1, r, tq, hd), lambda b, h, i: (b, h, 0, i, 0)),
            pl.BlockSpec((1, 1, hd, t), lambda b, h, i: (b, h, 0, 0)),
            pl.BlockSpec((1, 1, t, hd), lambda b, h, i: (b, h, 0, 0)),
        ],
        out_specs=pl.BlockSpec((1, 1, r, tq, hd), lambda b, h, i: (b, h, 0, i, 0)),
        out_shape=jax.ShapeDtypeStruct(qh.shape, _BF),
        compiler_params=_params(("arbitrary", "arbitrary", "arbitrary")),
        name="gqa_attention",
    )(qh, kt, vh)


def _diff_kernel(q_ref, kt_ref, v_ref, lq1_ref, lk1_ref, lq2_ref, lk2_ref, g_ref, o_ref, *, nct, tk, lam_init):
    tq = q_ref.shape[3]
    n_ctx = nct * TM
    n_lat = kt_ref.shape[-1] - n_ctx
    q1 = q_ref[0, 0, 0]
    q2 = q_ref[0, 0, 1]

    def step(carry, start, size):
        m1, l1, acc1, m2, l2, acc2 = carry
        v = v_ref[0, 0, pl.ds(start, size), :]
        p1, m1, alpha1, l1 = _softmax_step(_dot(q1, kt_ref[0, 0, 0, :, pl.ds(start, size)]), m1, l1)
        acc1 = alpha1 * acc1 + _dot(p1.astype(_BF), v)
        p2, m2, alpha2, l2 = _softmax_step(_dot(q2, kt_ref[0, 0, 1, :, pl.ds(start, size)]), m2, l2)
        acc2 = alpha2 * acc2 + _dot(p2.astype(_BF), v)
        return m1, l1, acc1, m2, l2, acc2

    one = (jnp.full((tq, 1), -jnp.inf, _F32), jnp.zeros((tq, 1), _F32), jnp.zeros((tq, DIFF_V_DIM), _F32))
    carry = step(one + one, 0, n_ctx)
    n_steps = jnp.where(pl.program_id(2) >= nct * (TM // tq), n_lat // tk, 0)
    carry = lax.fori_loop(
        0, n_steps, lambda j, c: step(c, pl.multiple_of(n_ctx + j * tk, math.gcd(n_ctx, tk)), tk), carry)
    m1, l1, acc1, m2, l2, acc2 = carry
    lam = (jnp.exp(jnp.sum(lq1_ref[...] * lk1_ref[...], axis=-1, keepdims=True))
           - jnp.exp(jnp.sum(lq2_ref[...] * lk2_ref[...], axis=-1, keepdims=True)) + lam_init)
    o = acc1 / l1 - lam * (acc2 / l2)
    o = o * lax.rsqrt(jnp.mean(o * o, axis=-1, keepdims=True) + EPS) * g_ref[...]
    o_ref[0] = (o * (1.0 - lam_init)).astype(o_ref.dtype)


def _diff_attention(dqh, dkt, dvh, lq1, lk1, lq2, lk2, subln_g, nct, tq, tk, lam_init):
    bsz, h, _, t, hd = dqh.shape
    vec = pl.BlockSpec((1, hd), lambda b, n, i: (0, 0))
    return pl.pallas_call(
        functools.partial(_diff_kernel, nct=nct, tk=tk, lam_init=lam_init),
        grid=(bsz, h, t // tq),
        in_specs=[
            pl.BlockSpec((1, 1, 2, tq, hd), lambda b, n, i: (b, n, 0, i, 0)),
            pl.BlockSpec((1, 1, 2, hd, t), lambda b, n, i: (b, n, 0, 0, 0)),
            pl.BlockSpec((1, 1, t, DIFF_V_DIM), lambda b, n, i: (b, n, 0, 0)),
            vec, vec, vec, vec,
            pl.BlockSpec((1, DIFF_V_DIM), lambda b, n, i: (0, 0)),
        ],
        out_specs=pl.BlockSpec((1, tq, DIFF_V_DIM), lambda b, n, i: (b, i, n)),
        out_shape=jax.ShapeDtypeStruct((bsz, t, h * DIFF_V_DIM), _BF),
        compiler_params=_params(("arbitrary", "arbitrary", "arbitrary")),
        name="diff_attention",
    )(dqh, dkt, dvh, lq1, lk1, lq2, lk2, subln_g)


def _merge_kernel(x_ref, mod_ref, g1_ref, wg_ref, bg_ref, yr_ref, ya_ref, yd_ref, wb_ref, wo_ref, o_ref):
    d = x_ref.shape[-1]
    x = x_ref[0]
    shift = mod_ref[0, :, 0:d]
    scale = mod_ref[0, :, d:2 * d]
    gate = mod_ref[0, :, 2 * d:3 * d]
    hn = _rms_mod(x, g1_ref[...], scale, shift).astype(_BF)
    ys = (yr_ref[0], ya_ref[0], yd_ref[0])
    m = None
    for n in range(N_BRANCH):
        g = jax.nn.sigmoid(_dot(hn, wg_ref[:, n * d:(n + 1) * d]) + bg_ref[:, n * d:(n + 1) * d])
        term = g * _dot(ys[n], wb_ref[n])
        m = term if m is None else m + term
    o_ref[0] = x + gate * _dot(m.astype(_BF), wo_ref[...])


def _merge(xa, mod3, g1, w_gate, b_gate, y_rec, y_gqa, y_diff, w_branch, w_out, nct):
    bsz, t, d = xa.shape
    nt = t // TM
    row = _mod_row(nct)
    tok = lambda w: pl.BlockSpec((1, TM, w), lambda b, i: (b, i, 0))
    bw = y_rec.shape[-1]
    return pl.pallas_call(
        _merge_kernel,
        grid=(bsz, nt),
        in_specs=[
            tok(d),
            pl.BlockSpec((1, 1, 3 * d), lambda b, i: (row(b, i, bsz), 0, 0)),
            pl.BlockSpec((1, d), lambda b, i: (0, 0)),
            _resident((d, N_BRANCH * d), lambda b, i: (0, 0)),
            pl.BlockSpec((1, N_BRANCH * d), lambda b, i: (0, 0)),
            tok(bw), tok(bw), tok(bw),
            _resident((N_BRANCH, bw, d), lambda b, i: (0, 0, 0)),
            _resident((d, d), lambda b, i: (0, 0)),
        ],
        out_specs=tok(d),
        out_shape=jax.ShapeDtypeStruct(xa.shape, _F32),
        compiler_params=_params(("arbitrary", "arbitrary")),
        name="merge",
    )(xa, mod3, g1, w_gate, b_gate, y_rec, y_gqa, y_diff, w_branch, w_out)


def _mlp_kernel(x_ref, mod_ref, g2_ref, wu_ref, wd_ref, gf_ref, o_ref, *, final):
    d = x_ref.shape[-1]
    x = x_ref[0]
    shift = mod_ref[0, :, 0:d]
    scale = mod_ref[0, :, d:2 * d]
    gate = mod_ref[0, :, 2 * d:3 * d]
    h = _rms_mod(x, g2_ref[...], scale, shift).astype(_BF)
    d_ff = wu_ref.shape[-1]
    acc = None
    for c in range(d_ff // d):
        u = jnp.maximum(_dot(h, wu_ref[:, c * d:(c + 1) * d]), 0.0)
        part = _dot((u * u).astype(_BF), wd_ref[c * d:(c + 1) * d, :])
        acc = part if acc is None else acc + part
    y = x + gate * acc
    if final:
        y = y * lax.rsqrt(jnp.mean(y * y, axis=-1, keepdims=True) + EPS) * gf_ref[...]
    o_ref[0] = y


def _mlp(xa, mod3, g2, w_up, w_down, final_g, nct, final):
    bsz, t, d = xa.shape
    nt = t // TM
    d_ff = w_up.shape[-1]
    row = _mod_row(nct)
    tok = pl.BlockSpec((1, TM, d), lambda b, i: (b, i, 0))
    return pl.pallas_call(
        functools.partial(_mlp_kernel, final=final),
        grid=(bsz, nt),
        in_specs=[
            tok,
            pl.BlockSpec((1, 1, 3 * d), lambda b, i: (row(b, i, bsz), 0, 1)),
            pl.BlockSpec((1, d), lambda b, i: (0, 0)),
            _resident((d, d_ff), lambda b, i: (0, 0)),
            _resident((d_ff, d), lambda b, i: (0, 0)),
            pl.BlockSpec((1, d), lambda b, i: (0, 0)),
        ],
        out_specs=tok,
        out_shape=jax.ShapeDtypeStruct(xa.shape, _F32),
        compiler_params=_params(("arbitrary", "arbitrary")),
        name="mlp",
    )(xa, mod3, g2, w_up, w_down, final_g)


def _rope_tables(n_ctx, s):
    rows = s // GRID_W
    pos_r = jnp.repeat(jnp.arange(rows, dtype=_F32), GRID_W)
    pos_c = jnp.tile(jnp.arange(GRID_W, dtype=_F32), rows)
    n_freq = HEAD_DIM // 4
    inv = ROPE_THETA ** (-jnp.arange(n_freq, dtype=_F32) * 2.0 / (HEAD_DIM // 2))
    ang_r = pos_r[:, None] * inv
    ang_c = pos_c[:, None] * inv
    ang = jnp.concatenate([ang_r, ang_r, ang_c, ang_c], axis=-1)
    sign = jnp.tile(jnp.concatenate([-jnp.ones(n_freq, _F32), jnp.ones(n_freq, _F32)]), 2)
    cos = jnp.concatenate([jnp.ones((n_ctx, HEAD_DIM), _F32), jnp.cos(ang)], axis=0)
    sin = jnp.concatenate([jnp.zeros((n_ctx, HEAD_DIM), _F32), jnp.sin(ang) * sign], axis=0)
    return jnp.tile(cos, (1, LANES // HEAD_DIM)), jnp.tile(sin, (1, LANES // HEAD_DIM))


def _block_diag(w):
    nb, n, _ = w.shape
    eye = jnp.eye(nb, dtype=w.dtype)
    return jnp.einsum('ncd,nm->ncmd', w, eye).reshape(nb * n, nb * n)


def _forward(x, c, ctx, c_ctx, w_mod, b_mod, norm1_g, w_in, b_gate, conv_w, conv_b, w_rg, b_rg, w_ig, b_ig,
             lru_lambda, q_norm_g, k_norm_g, lambda_q1, lambda_k1, lambda_q2, lambda_k2, subln_g, w_branch,
             w_out, norm2_g, w_up, w_down, final_g, *, tq_gqa, tq_diff, tk):
    bsz, s, d = x.shape
    n_ctx = ctx.shape[1]
    depth = w_mod.shape[0]
    assert n_ctx % TM == 0 and s % TM == 0 and s % tk == 0 and bsz < MOD_ROWS
    nct = n_ctx // TM
    t = n_ctx + s

    cin = jnp.zeros((MOD_ROWS, d), _F32).at[:bsz].set(c).at[bsz].set(c_ctx)
    mod = _modulation(cin, w_mod, b_mod)
    cos, sin = _rope_tables(n_ctx, s)
    bd = jnp.kron(jnp.eye(LANES // HEAD_DIM, dtype=_F32), jnp.ones((HEAD_DIM, HEAD_DIM), _F32)).astype(_BF)
    xa = jnp.concatenate([ctx, x], axis=1)

    for l in range(depth):
        last = l == depth - 1
        lam_init = 0.8 - 0.6 * math.exp(-0.3 * l)
        mod3 = mod[l].reshape(MOD_ROWS, 1, 6 * d)
        w_proj = w_in[l, :, :N_PROJ].astype(_BF)
        w_gate = w_in[l, :, N_PROJ:].astype(_BF)
        gq = jnp.tile(q_norm_g[l], LANES // HEAD_DIM).reshape(1, LANES)
        gk = jnp.tile(k_norm_g[l], LANES // HEAD_DIM).reshape(1, LANES)
        z0, gz1, q, k, v, dq, dk, dv = _inproj(xa, mod3, norm1_g[l].reshape(1, d), w_proj, gq, gk, cos, sin, bd, nct)

        lru = lambda dr, rev, **kw: _lru_sweep(
            z0, conv_w[l], conv_b[l].reshape(1, -1), _block_diag(w_rg[l, dr]).astype(_BF), b_rg[l, dr].reshape(1, -1),
            _block_diag(w_ig[l, dr]).astype(_BF), b_ig[l, dr].reshape(1, -1), lru_lambda[l, dr].reshape(1, -1),
            nct, rev, **kw)
        hf = lru(0, False)
        y_rec = lru(1, True, hf=hf, gz1=gz1)

        qh = q.reshape(bsz, t, GQA_KV_HEADS, GQA_GROUP, HEAD_DIM).transpose(0, 2, 3, 1, 4)
        kt = k.reshape(bsz, t, GQA_KV_HEADS, HEAD_DIM).transpose(0, 2, 3, 1)
        vh = v.reshape(bsz, t, GQA_KV_HEADS, HEAD_DIM).transpose(0, 2, 1, 3)
        oh = _gqa_attention(qh, kt, vh, nct, tq_gqa, tk)
        y_gqa = oh.transpose(0, 3, 1, 2, 4).reshape(bsz, t, GQA_HEADS * HEAD_DIM)

        dqh = dq.reshape(bsz, t, DIFF_HEADS, 2, HEAD_DIM).transpose(0, 2, 3, 1, 4)
        dkt = dk.reshape(bsz, t, DIFF_HEADS, 2, HEAD_DIM).transpose(0, 2, 3, 4, 1)
        dvh = dv.reshape(bsz, t, DIFF_HEADS, DIFF_V_DIM).transpose(0, 2, 1, 3)
        vec = lambda a: a[l].reshape(1, HEAD_DIM)
        y_diff = _diff_attention(dqh, dkt, dvh, vec(lambda_q1), vec(lambda_k1), vec(lambda_q2), vec(lambda_k2),
                                 subln_g[l].reshape(1, DIFF_V_DIM), nct, tq_diff, tk, lam_init)

        xa = _merge(xa, mod3, norm1_g[l].reshape(1, d), w_gate, b_gate[l].reshape(1, -1), y_rec, y_gqa, y_diff,
                    w_branch[l].astype(_BF), w_out[l].astype(_BF), nct)
        xa = _mlp(xa, mod3, norm2_g[l].reshape(1, d), w_up[l].astype(_BF), w_down[l].astype(_BF),
                  final_g.reshape(1, d), nct, last)
    return xa[:, n_ctx:]


def kernel(x, c, ctx, c_ctx, w_mod, b_mod, norm1_g, w_in, b_gate, conv_w, conv_b, w_rg, b_rg, w_ig, b_ig, lru_lambda, q_norm_g, k_norm_g, lambda_q1, lambda_k1, lambda_q2, lambda_k2, subln_g, w_branch, w_out, norm2_g, w_up, w_down, final_g):
    return _forward(x, c, ctx, c_ctx, w_mod, b_mod, norm1_g, w_in, b_gate, conv_w, conv_b, w_rg, b_rg, w_ig, b_ig,
                    lru_lambda, q_norm_g, k_norm_g, lambda_q1, lambda_k1, lambda_q2, lambda_k2, subln_g, w_branch,
                    w_out, norm2_g, w_up, w_down, final_g, tq_gqa=128, tq_diff=256, tk=512)
```

```python
import functools
import math

import jax
import jax.numpy as jnp
from jax import lax
from jax.experimental import pallas as pl
from jax.experimental.pallas import tpu as pltpu

GRID_W = 64
HEAD_DIM = 64
LRU_WIDTH = 512
LRU_BLOCKS = 8
CONV_W = 4
CONV_LEFT = 2
LRU_C = 8.0
GQA_HEADS = 8
GQA_KV_HEADS = 2
GQA_GROUP = GQA_HEADS // GQA_KV_HEADS
DIFF_HEADS = 4
DIFF_V_DIM = 2 * HEAD_DIM
N_BRANCH = 3
ROPE_THETA = 10000.0
EPS = 1e-6
SCALE = HEAD_DIM ** -0.5

TM = 256
HALO = 8
LANES = 128
MOD_ROWS = 8
VMEM_LIMIT = 56 * 1024 * 1024

N_PROJ = 2 * LRU_WIDTH + (GQA_HEADS + 2 * GQA_KV_HEADS) * HEAD_DIM + DIFF_HEADS * (4 * HEAD_DIM + DIFF_V_DIM)

_BF = jnp.bfloat16
_F32 = jnp.float32


def _params(sem):
    return pltpu.CompilerParams(dimension_semantics=sem, vmem_limit_bytes=VMEM_LIMIT)


def _resident(shape, index_map):
    return pl.BlockSpec(shape, index_map, pipeline_mode=pl.Buffered(1))


def _split_bf16(a):
    hi = a.astype(_BF)
    lo = (a - hi.astype(_F32)).astype(_BF)
    return hi, lo


def _dot(a, b):
    return jnp.dot(a, b, preferred_element_type=_F32)


def _mod_kernel(c_ref, w_ref, b_ref, o_ref):
    c = c_ref[...]
    s = c * jax.nn.sigmoid(c)
    s_hi, s_lo = _split_bf16(s)
    w_hi, w_lo = _split_bf16(w_ref[...])
    o_ref[...] = _dot(s_hi, w_hi) + (_dot(s_lo, w_hi) + _dot(s_hi, w_lo)) + b_ref[...]


def _modulation(cin, w_mod, b_mod):
    depth, d, n = w_mod.shape
    tn = 1536
    return pl.pallas_call(
        _mod_kernel,
        grid=(depth, n // tn),
        in_specs=[
            pl.BlockSpec((MOD_ROWS, d), lambda l, j: (0, 0)),
            pl.BlockSpec((None, d, tn), lambda l, j: (l, 0, j)),
            pl.BlockSpec((None, 1, tn), lambda l, j: (l, 0, j)),
        ],
        out_specs=pl.BlockSpec((None, MOD_ROWS, tn), lambda l, j: (l, 0, j)),
        out_shape=jax.ShapeDtypeStruct((depth, MOD_ROWS, n), _F32),
        compiler_params=_params(("arbitrary", "arbitrary")),
        name="modulation",
    )(cin, w_mod, b_mod.reshape(depth, 1, n))


def _rms_mod(x, g, scale, shift):
    y = x * lax.rsqrt(jnp.mean(x * x, axis=-1, keepdims=True) + EPS) * g
    return y * (1.0 + scale) + shift


def _head_sumsq(c, bd):
    hi, lo = _split_bf16(c * c)
    return _dot(hi, bd) + _dot(lo, bd)


def _swap16(y):
    lane = lax.broadcasted_iota(jnp.int32, y.shape, 1)
    return jnp.where(lane % 32 < 16, pltpu.roll(y, LANES - 16, 1), pltpu.roll(y, 16, 1))


def _rope(y, cos, sin):
    return y * cos + _swap16(y) * sin


def _inproj_kernel(x_ref, mod_ref, g1_ref, w_ref, gq_ref, gk_ref, cos_ref, sin_ref, bd_ref,
                   z0_ref, gz1_ref, q_ref, k_ref, v_ref, dq_ref, dk_ref, dv_ref):
    d = x_ref.shape[-1]
    shift = mod_ref[0, :, 0:d]
    scale = mod_ref[0, :, d:2 * d]
    hn = _rms_mod(x_ref[0], g1_ref[...], scale, shift).astype(_BF)
    cos = cos_ref[...]
    sin = sin_ref[...]
    bd = bd_ref[...]

    def proj(a, b):
        return _dot(hn, w_ref[:, a:b])

    c0 = 0
    z0_ref[0] = proj(c0, c0 + LRU_WIDTH)
    c0 += LRU_WIDTH
    gz1_ref[0] = jax.nn.gelu(proj(c0, c0 + LRU_WIDTH))
    c0 += LRU_WIDTH
    for j in range(GQA_HEADS * HEAD_DIM // LANES):
        c = proj(c0 + j * LANES, c0 + (j + 1) * LANES)
        y = c * lax.rsqrt(_head_sumsq(c, bd) * (1.0 / HEAD_DIM) + EPS) * gq_ref[...]
        q_ref[0, :, j * LANES:(j + 1) * LANES] = (_rope(y, cos, sin) * SCALE).astype(_BF)
    c0 += GQA_HEADS * HEAD_DIM
    c = proj(c0, c0 + LANES)
    y = c * lax.rsqrt(_head_sumsq(c, bd) * (1.0 / HEAD_DIM) + EPS) * gk_ref[...]
    k_ref[0] = _rope(y, cos, sin).astype(_BF)
    c0 += GQA_KV_HEADS * HEAD_DIM
    v_ref[0] = proj(c0, c0 + LANES).astype(_BF)
    c0 += GQA_KV_HEADS * HEAD_DIM
    for j in range(DIFF_HEADS * 2 * HEAD_DIM // LANES):
        c = proj(c0 + j * LANES, c0 + (j + 1) * LANES)
        dq_ref[0, :, j * LANES:(j + 1) * LANES] = (_rope(c, cos, sin) * SCALE).astype(_BF)
    c0 += DIFF_HEADS * 2 * HEAD_DIM
    for j in range(DIFF_HEADS * 2 * HEAD_DIM // LANES):
        c = proj(c0 + j * LANES, c0 + (j + 1) * LANES)
        dk_ref[0, :, j * LANES:(j + 1) * LANES] = _rope(c, cos, sin).astype(_BF)
    c0 += DIFF_HEADS * 2 * HEAD_DIM
    dv_ref[0] = proj(c0, c0 + DIFF_HEADS * DIFF_V_DIM).astype(_BF)


def _mod_row(nct):
    def row(b, t, nb):
        return jnp.where(t < nct, nb, b)
    return row


def _inproj(xa, mod3, g1, w_proj, gq, gk, cos, sin, bd, nct):
    bsz, t, d = xa.shape
    nt = t // TM
    row = _mod_row(nct)
    tok = lambda w: pl.BlockSpec((1, TM, w), lambda b, i: (b, i, 0))
    const = lambda shape: pl.BlockSpec(shape, lambda b, i: (0,) * len(shape))
    outs = [(LRU_WIDTH, _F32), (LRU_WIDTH, _F32), (GQA_HEADS * HEAD_DIM, _BF), (GQA_KV_HEADS * HEAD_DIM, _BF),
            (GQA_KV_HEADS * HEAD_DIM, _BF), (DIFF_HEADS * 2 * HEAD_DIM, _BF), (DIFF_HEADS * 2 * HEAD_DIM, _BF),
            (DIFF_HEADS * DIFF_V_DIM, _BF)]
    return pl.pallas_call(
        _inproj_kernel,
        grid=(bsz, nt),
        in_specs=[
            tok(d),
            pl.BlockSpec((1, 1, 2 * d), lambda b, i: (row(b, i, bsz), 0, 0)),
            const((1, d)),
            _resident((d, N_PROJ), lambda b, i: (0, 0)),
            const((1, LANES)), const((1, LANES)),
            pl.BlockSpec((TM, LANES), lambda b, i: (i, 0)),
            pl.BlockSpec((TM, LANES), lambda b, i: (i, 0)),
            const((LANES, LANES)),
        ],
        out_specs=[tok(w) for w, _ in outs],
        out_shape=[jax.ShapeDtypeStruct((bsz, t, w), dt) for w, dt in outs],
        compiler_params=_params(("arbitrary", "arbitrary")),
        name="inproj",
    )(xa, mod3, g1, w_proj, gq, gk, cos, sin, bd)


def _lru_kernel(*refs, nct, nt, reverse):
    if reverse:
        (z_ref, zp_ref, zn_ref, cw_ref, cb_ref, wr_ref, br_ref, wi_ref, bi_ref, lam_ref, hf_ref, gz1_ref,
         o_ref, ext_ref, a_ref, b_ref, h_ref) = refs
    else:
        (z_ref, zp_ref, zn_ref, cw_ref, cb_ref, wr_ref, br_ref, wi_ref, bi_ref, lam_ref,
         o_ref, ext_ref, a_ref, b_ref, h_ref) = refs
    i = pl.program_id(1)
    tile = _lru_tile(i, nct, nt, reverse)

    @pl.when(i == 0)
    def _():
        h_ref[...] = jnp.zeros_like(h_ref)

    seg_first = jnp.logical_or(tile == 0, tile == nct)
    seg_last = jnp.logical_or(tile == nct - 1, tile == nt - 1)
    u = z_ref[0]
    ext_ref[0:HALO, :] = jnp.where(seg_first, 0.0, zp_ref[0])
    ext_ref[HALO:HALO + TM, :] = u
    ext_ref[HALO + TM:, :] = jnp.where(seg_last, 0.0, zn_ref[0])
    y = cb_ref[...] + cw_ref[CONV_LEFT:CONV_LEFT + 1, :] * u
    for j in range(CONV_W):
        if j != CONV_LEFT:
            y = y + cw_ref[j:j + 1, :] * ext_ref[pl.ds(HALO + j - CONV_LEFT, TM), :]

    yb = y.astype(_BF)
    r = jax.nn.sigmoid(_dot(yb, wr_ref[...]) + br_ref[...])
    ig = jax.nn.sigmoid(_dot(yb, wi_ref[...]) + bi_ref[...])
    log_a = (-LRU_C) * r * jax.nn.softplus(-lam_ref[...])
    a = jnp.exp(log_a)
    a_ref[...] = a
    b_ref[...] = jnp.sqrt(-jnp.tanh(log_a) * (a * a + 1.0)) * (ig * y)

    ng = TM // 8
    row = lax.broadcasted_iota(jnp.int32, (8, LRU_WIDTH), 0)

    def body(g, h):
        r0 = pl.multiple_of((ng - 1 - g if reverse else g) * 8, 8)
        aa = a_ref[pl.ds(r0, 8), :]
        bb = b_ref[pl.ds(r0, 8), :]
        for s in (1, 2, 4):
            sh = 8 - s if reverse else s
            m = (row < 8 - s) if reverse else (row >= s)
            bb = jnp.where(m, aa * pltpu.roll(bb, sh, 0) + bb, bb)
            aa = jnp.where(m, aa * pltpu.roll(aa, sh, 0), aa)
        hh = aa * h + bb
        if reverse:
            hsum = hf_ref[0, pl.ds(r0, 8), :] + hh
            o_ref[0, pl.ds(r0, 8), :] = (hsum * gz1_ref[0, pl.ds(r0, 8), :]).astype(o_ref.dtype)
            return hh[0:1, :]
        o_ref[0, pl.ds(r0, 8), :] = hh
        return hh[7:8, :]

    h_ref[0:1, :] = lax.fori_loop(0, ng, body, h_ref[0:1, :])


def _lru_tile(i, nct, nt, reverse):
    if not reverse:
        return i
    return jnp.where(i < nct, nct - 1 - i, nt - 1 - (i - nct))


def _lru_sweep(z0, conv_w, conv_b, w_r, b_r, w_i, b_i, lam, nct, reverse, hf=None, gz1=None):
    bsz, t, w = z0.shape
    nt = t // TM
    hb = TM // HALO
    tile = lambda i: _lru_tile(i, nct, nt, reverse)
    tok = pl.BlockSpec((1, TM, w), lambda b, i: (b, tile(i), 0))
    const = lambda shape: pl.BlockSpec(shape, lambda b, i: (0,) * len(shape))
    in_specs = [
        tok,
        pl.BlockSpec((1, HALO, w), lambda b, i: (b, jnp.maximum(tile(i) * hb - 1, 0), 0)),
        pl.BlockSpec((1, HALO, w), lambda b, i: (b, jnp.minimum((tile(i) + 1) * hb, nt * hb - 1), 0)),
        const((CONV_W, w)), const((1, w)), const((w, w)), const((1, w)), const((w, w)), const((1, w)), const((1, w)),
    ]
    args = [z0, z0, z0, conv_w, conv_b, w_r, b_r, w_i, b_i, lam]
    if reverse:
        in_specs += [tok, tok]
        args += [hf, gz1]
    return pl.pallas_call(
        functools.partial(_lru_kernel, nct=nct, nt=nt, reverse=reverse),
        grid=(bsz, nt),
        in_specs=in_specs,
        out_specs=tok,
        out_shape=jax.ShapeDtypeStruct((bsz, t, w), _BF if reverse else _F32),
        scratch_shapes=[pltpu.VMEM((TM + 2 * HALO, w), _F32), pltpu.VMEM((TM, w), _F32),
                        pltpu.VMEM((TM, w), _F32), pltpu.VMEM((8, w), _F32)],
        compiler_params=_params(("arbitrary", "arbitrary")),
        name="lru_bwd" if reverse else "lru_fwd",
    )(*args)


SAFE_LOGIT = 60.0
SAFE_VALUE = 1.0e6


def _attend(pairs, v_at, is_latent, n_ctx, n_lat, tk, v_width, stabilize):
    rows = pairs[0][0].shape[0]

    def step(carry, start, size):
        v = v_at(start, size)
        out = []
        for (q, kt_at), state in zip(pairs, carry):
            s = _dot(q, kt_at(start, size))
            if stabilize:
                m, acc = state
                m_new = jnp.maximum(m, jnp.max(s, axis=-1, keepdims=True))
                acc = jnp.exp(m - m_new) * acc + _dot(jnp.exp(s - m_new).astype(_BF), v)
                out.append((m_new, acc))
            else:
                out.append((state[0] + _dot(jnp.exp(s).astype(_BF), v),))
        return tuple(out)

    zero = jnp.zeros((rows, v_width), _F32)
    init = (jnp.full((rows, 1), -jnp.inf, _F32), zero) if stabilize else (zero,)
    init = (init,) * len(pairs)

    def all_keys():
        carry = step(init, 0, n_ctx + tk)
        return lax.fori_loop(
            1, n_lat // tk, lambda j, c: step(c, pl.multiple_of(n_ctx + j * tk, math.gcd(n_ctx, tk)), tk), carry)

    carry = lax.cond(is_latent, all_keys, lambda: step(init, 0, n_ctx))
    return [state[-1] for state in carry]


def _gqa_kernel(q_ref, kt_ref, v_ref, o_ref, *, nct, tk, stabilize):
    tq = q_ref.shape[3]
    n_ctx = nct * TM
    q = q_ref[0, 0].reshape(GQA_GROUP * tq, HEAD_DIM)
    (acc,) = _attend(
        [(q, lambda a, n: kt_ref[0, 0, :, pl.ds(a, n)])], lambda a, n: v_ref[0, 0, pl.ds(a, n), :],
        pl.program_id(2) >= n_ctx // tq, n_ctx, kt_ref.shape[-1] - n_ctx, tk, v_ref.shape[-1], stabilize)
    o = acc[:, :HEAD_DIM] / acc[:, HEAD_DIM:HEAD_DIM + 1]
    o_ref[0, 0] = o.reshape(GQA_GROUP, tq, HEAD_DIM).astype(o_ref.dtype)


def _gqa_attention(qh, kt, vh, *, nct, tq, tk, stabilize):
    bsz, g, r, t, hd = qh.shape
    vw = vh.shape[-1]
    return pl.pallas_call(
        functools.partial(_gqa_kernel, nct=nct, tk=tk, stabilize=stabilize),
        grid=(bsz, g, t // tq),
        in_specs=[
            pl.BlockSpec((1, 1, r, tq, hd), lambda b, h, i: (b, h, 0, i, 0)),
            pl.BlockSpec((1, 1, hd, t), lambda b, h, i: (b, h, 0, 0)),
            pl.BlockSpec((1, 1, t, vw), lambda b, h, i: (b, h, 0, 0)),
        ],
        out_specs=pl.BlockSpec((1, 1, r, tq, hd), lambda b, h, i: (b, h, 0, i, 0)),
        out_shape=jax.ShapeDtypeStruct(qh.shape, _BF),
        compiler_params=_params(("arbitrary", "arbitrary", "arbitrary")),
        name="gqa_attention_stable" if stabilize else "gqa_attention",
    )(qh, kt, vh)


def _diff_kernel(q_ref, kt_ref, v_ref, lq1_ref, lk1_ref, lq2_ref, lk2_ref, g_ref, o_ref, *,
                 nct, tk, lam_init, stabilize):
    tq = q_ref.shape[3]
    n_ctx = nct * TM
    pairs = [(q_ref[0, 0, n], functools.partial(lambda n, a, size: kt_ref[0, 0, n, :, pl.ds(a, size)], n))
             for n in range(2)]
    acc1, acc2 = _attend(
        pairs, lambda a, n: v_ref[0, 0, pl.ds(a, n), :],
        pl.program_id(2) >= n_ctx // tq, n_ctx, kt_ref.shape[-1] - n_ctx, tk, v_ref.shape[-1], stabilize)
    lam = (jnp.exp(jnp.sum(lq1_ref[...] * lk1_ref[...], axis=-1, keepdims=True))
           - jnp.exp(jnp.sum(lq2_ref[...] * lk2_ref[...], axis=-1, keepdims=True)) + lam_init)
    o = (acc1[:, :DIFF_V_DIM] / acc1[:, DIFF_V_DIM:DIFF_V_DIM + 1]
         - lam * (acc2[:, :DIFF_V_DIM] / acc2[:, DIFF_V_DIM:DIFF_V_DIM + 1]))
    o = o * lax.rsqrt(jnp.mean(o * o, axis=-1, keepdims=True) + EPS) * g_ref[...]
    o_ref[0] = (o * (1.0 - lam_init)).astype(o_ref.dtype)


def _diff_attention(dqh, dkt, dvh, lq1, lk1, lq2, lk2, subln_g, *, nct, tq, tk, lam_init, stabilize):
    bsz, h, _, t, hd = dqh.shape
    vw = dvh.shape[-1]
    vec = pl.BlockSpec((1, hd), lambda b, n, i: (0, 0))
    return pl.pallas_call(
        functools.partial(_diff_kernel, nct=nct, tk=tk, lam_init=lam_init, stabilize=stabilize),
        grid=(bsz, h, t // tq),
        in_specs=[
            pl.BlockSpec((1, 1, 2, tq, hd), lambda b, n, i: (b, n, 0, i, 0)),
            pl.BlockSpec((1, 1, 2, hd, t), lambda b, n, i: (b, n, 0, 0, 0)),
            pl.BlockSpec((1, 1, t, vw), lambda b, n, i: (b, n, 0, 0)),
            vec, vec, vec, vec,
            pl.BlockSpec((1, DIFF_V_DIM), lambda b, n, i: (0, 0)),
        ],
        out_specs=pl.BlockSpec((1, tq, DIFF_V_DIM), lambda b, n, i: (b, i, n)),
        out_shape=jax.ShapeDtypeStruct((bsz, t, h * DIFF_V_DIM), _BF),
        compiler_params=_params(("arbitrary", "arbitrary", "arbitrary")),
        name="diff_attention_stable" if stabilize else "diff_attention",
    )(dqh, dkt, dvh, lq1, lk1, lq2, lk2, subln_g)


def _logits_bounded(qh, kt, vh):
    qn = jnp.max(jnp.sum(jnp.square(qh.astype(_F32)), axis=-1))
    kn = jnp.max(jnp.sum(jnp.square(kt.astype(_F32)), axis=-2))
    vmax = jnp.max(jnp.abs(vh.astype(_F32)))
    return jnp.logical_and(qn * kn <= SAFE_LOGIT * SAFE_LOGIT, vmax <= SAFE_VALUE)


def _with_ones(v, width):
    pad = width - v.shape[-1] - 1
    return jnp.concatenate([v, jnp.ones(v.shape[:-1] + (1,), v.dtype), jnp.zeros(v.shape[:-1] + (pad,), v.dtype)], -1)


def _merge_kernel(x_ref, mod_ref, g1_ref, wg_ref, bg_ref, yr_ref, ya_ref, yd_ref, wb_ref, wo_ref, o_ref):
    d = x_ref.shape[-1]
    x = x_ref[0]
    shift = mod_ref[0, :, 0:d]
    scale = mod_ref[0, :, d:2 * d]
    gate = mod_ref[0, :, 2 * d:3 * d]
    hn = _rms_mod(x, g1_ref[...], scale, shift).astype(_BF)
    ys = (yr_ref[0], ya_ref[0], yd_ref[0])
    m = None
    for n in range(N_BRANCH):
        g = jax.nn.sigmoid(_dot(hn, wg_ref[:, n * d:(n + 1) * d]) + bg_ref[:, n * d:(n + 1) * d])
        term = g * _dot(ys[n], wb_ref[n])
        m = term if m is None else m + term
    o_ref[0] = x + gate * _dot(m.astype(_BF), wo_ref[...])


def _merge(xa, mod3, g1, w_gate, b_gate, y_rec, y_gqa, y_diff, w_branch, w_out, nct):
    bsz, t, d = xa.shape
    nt = t // TM
    row = _mod_row(nct)
    tok = lambda w: pl.BlockSpec((1, TM, w), lambda b, i: (b, i, 0))
    bw = y_rec.shape[-1]
    return pl.pallas_call(
        _merge_kernel,
        grid=(bsz, nt),
        in_specs=[
            tok(d),
            pl.BlockSpec((1, 1, 3 * d), lambda b, i: (row(b, i, bsz), 0, 0)),
            pl.BlockSpec((1, d), lambda b, i: (0, 0)),
            _resident((d, N_BRANCH * d), lambda b, i: (0, 0)),
            pl.BlockSpec((1, N_BRANCH * d), lambda b, i: (0, 0)),
            tok(bw), tok(bw), tok(bw),
            _resident((N_BRANCH, bw, d), lambda b, i: (0, 0, 0)),
            _resident((d, d), lambda b, i: (0, 0)),
        ],
        out_specs=tok(d),
        out_shape=jax.ShapeDtypeStruct(xa.shape, _F32),
        compiler_params=_params(("arbitrary", "arbitrary")),
        name="merge",
    )(xa, mod3, g1, w_gate, b_gate, y_rec, y_gqa, y_diff, w_branch, w_out)


def _mlp_kernel(x_ref, mod_ref, g2_ref, wu_ref, wd_ref, gf_ref, o_ref, *, final):
    d = x_ref.shape[-1]
    x = x_ref[0]
    shift = mod_ref[0, :, 0:d]
    scale = mod_ref[0, :, d:2 * d]
    gate = mod_ref[0, :, 2 * d:3 * d]
    h = _rms_mod(x, g2_ref[...], scale, shift).astype(_BF)
    d_ff = wu_ref.shape[-1]
    acc = None
    for c in range(d_ff // d):
        u = jnp.maximum(_dot(h, wu_ref[:, c * d:(c + 1) * d]), 0.0)
        part = _dot((u * u).astype(_BF), wd_ref[c * d:(c + 1) * d, :])
        acc = part if acc is None else acc + part
    y = x + gate * acc
    if final:
        y = y * lax.rsqrt(jnp.mean(y * y, axis=-1, keepdims=True) + EPS) * gf_ref[...]
    o_ref[0] = y


def _mlp(xa, mod3, g2, w_up, w_down, final_g, nct, final):
    bsz, t, d = xa.shape
    nt = t // TM
    d_ff = w_up.shape[-1]
    row = _mod_row(nct)
    tok = pl.BlockSpec((1, TM, d), lambda b, i: (b, i, 0))
    return pl.pallas_call(
        functools.partial(_mlp_kernel, final=final),
        grid=(bsz, nt),
        in_specs=[
            tok,
            pl.BlockSpec((1, 1, 3 * d), lambda b, i: (row(b, i, bsz), 0, 1)),
            pl.BlockSpec((1, d), lambda b, i: (0, 0)),
            _resident((d, d_ff), lambda b, i: (0, 0)),
            _resident((d_ff, d), lambda b, i: (0, 0)),
            pl.BlockSpec((1, d), lambda b, i: (0, 0)),
        ],
        out_specs=tok,
        out_shape=jax.ShapeDtypeStruct(xa.shape, _F32),
        compiler_params=_params(("arbitrary", "arbitrary")),
        name="mlp",
    )(xa, mod3, g2, w_up, w_down, final_g)


def _rope_tables(n_ctx, s):
    rows = s // GRID_W
    pos_r = jnp.repeat(jnp.arange(rows, dtype=_F32), GRID_W)
    pos_c = jnp.tile(jnp.arange(GRID_W, dtype=_F32), rows)
    n_freq = HEAD_DIM // 4
    inv = ROPE_THETA ** (-jnp.arange(n_freq, dtype=_F32) * 2.0 / (HEAD_DIM // 2))
    ang_r = pos_r[:, None] * inv
    ang_c = pos_c[:, None] * inv
    ang = jnp.concatenate([ang_r, ang_r, ang_c, ang_c], axis=-1)
    sign = jnp.tile(jnp.concatenate([-jnp.ones(n_freq, _F32), jnp.ones(n_freq, _F32)]), 2)
    cos = jnp.concatenate([jnp.ones((n_ctx, HEAD_DIM), _F32), jnp.cos(ang)], axis=0)
    sin = jnp.concatenate([jnp.zeros((n_ctx, HEAD_DIM), _F32), jnp.sin(ang) * sign], axis=0)
    return jnp.tile(cos, (1, LANES // HEAD_DIM)), jnp.tile(sin, (1, LANES // HEAD_DIM))


def _block_diag(w):
    nb, n, _ = w.shape
    eye = jnp.eye(nb, dtype=w.dtype)
    return jnp.einsum('ncd,nm->ncmd', w, eye).reshape(nb * n, nb * n)


def _forward(x, c, ctx, c_ctx, w_mod, b_mod, norm1_g, w_in, b_gate, conv_w, conv_b, w_rg, b_rg, w_ig, b_ig,
             lru_lambda, q_norm_g, k_norm_g, lambda_q1, lambda_k1, lambda_q2, lambda_k2, subln_g, w_branch,
             w_out, norm2_g, w_up, w_down, final_g, *, tq_gqa, tq_diff, tk):
    bsz, s, d = x.shape
    n_ctx = ctx.shape[1]
    depth = w_mod.shape[0]
    assert n_ctx % TM == 0 and s % TM == 0 and s % tk == 0 and bsz < MOD_ROWS
    assert TM % tq_gqa == 0 and TM % tq_diff == 0
    nct = n_ctx // TM
    t = n_ctx + s

    cin = jnp.zeros((MOD_ROWS, d), _F32).at[:bsz].set(c).at[bsz].set(c_ctx)
    mod = _modulation(cin, w_mod, b_mod)
    cos, sin = _rope_tables(n_ctx, s)
    bd = jnp.kron(jnp.eye(LANES // HEAD_DIM, dtype=_F32), jnp.ones((HEAD_DIM, HEAD_DIM), _F32)).astype(_BF)
    xa = jnp.concatenate([ctx, x], axis=1)

    for l in range(depth):
        last = l == depth - 1
        lam_init = 0.8 - 0.6 * math.exp(-0.3 * l)
        mod3 = mod[l].reshape(MOD_ROWS, 1, 6 * d)
        w_proj = w_in[l, :, :N_PROJ].astype(_BF)
        w_gate = w_in[l, :, N_PROJ:].astype(_BF)
        gq = jnp.tile(q_norm_g[l], LANES // HEAD_DIM).reshape(1, LANES)
        gk = jnp.tile(k_norm_g[l], LANES // HEAD_DIM).reshape(1, LANES)
        z0, gz1, q, k, v, dq, dk, dv = _inproj(xa, mod3, norm1_g[l].reshape(1, d), w_proj, gq, gk, cos, sin, bd, nct)

        lru = lambda dr, rev, **kw: _lru_sweep(
            z0, conv_w[l], conv_b[l].reshape(1, -1), _block_diag(w_rg[l, dr]).astype(_BF), b_rg[l, dr].reshape(1, -1),
            _block_diag(w_ig[l, dr]).astype(_BF), b_ig[l, dr].reshape(1, -1), lru_lambda[l, dr].reshape(1, -1),
            nct, rev, **kw)
        hf = lru(0, False)
        y_rec = lru(1, True, hf=hf, gz1=gz1)

        qh = q.reshape(bsz, t, GQA_KV_HEADS, GQA_GROUP, HEAD_DIM).transpose(0, 2, 3, 1, 4)
        kt = k.reshape(bsz, t, GQA_KV_HEADS, HEAD_DIM).transpose(0, 2, 3, 1)
        vh = _with_ones(v.reshape(bsz, t, GQA_KV_HEADS, HEAD_DIM).transpose(0, 2, 1, 3), LANES)
        gqa = lambda stab: functools.partial(_gqa_attention, nct=nct, tq=tq_gqa, tk=tk if stab else s, stabilize=stab)
        oh = lax.cond(_logits_bounded(qh, kt, vh), gqa(False), gqa(True), qh, kt, vh)
        y_gqa = oh.transpose(0, 3, 1, 2, 4).reshape(bsz, t, GQA_HEADS * HEAD_DIM)

        dqh = dq.reshape(bsz, t, DIFF_HEADS, 2, HEAD_DIM).transpose(0, 2, 3, 1, 4)
        dkt = dk.reshape(bsz, t, DIFF_HEADS, 2, HEAD_DIM).transpose(0, 2, 3, 4, 1)
        dvh = _with_ones(dv.reshape(bsz, t, DIFF_HEADS, DIFF_V_DIM).transpose(0, 2, 1, 3), 2 * LANES)
        vec = lambda a: a[l].reshape(1, HEAD_DIM)
        diff = lambda stab: functools.partial(_diff_attention, nct=nct, tq=tq_diff, tk=tk if stab else s,
                                              lam_init=lam_init, stabilize=stab)
        y_diff = lax.cond(_logits_bounded(dqh, dkt, dvh), diff(False), diff(True),
                          dqh, dkt, dvh, vec(lambda_q1), vec(lambda_k1), vec(lambda_q2), vec(lambda_k2),
                          subln_g[l].reshape(1, DIFF_V_DIM))

        xa = _merge(xa, mod3, norm1_g[l].reshape(1, d), w_gate, b_gate[l].reshape(1, -1), y_rec, y_gqa, y_diff,
                    w_branch[l].astype(_BF), w_out[l].astype(_BF), nct)
        xa = _mlp(xa, mod3, norm2_g[l].reshape(1, d), w_up[l].astype(_BF), w_down[l].astype(_BF),
                  final_g.reshape(1, d), nct, last)
    return xa[:, n_ctx:]


def kernel(x, c, ctx, c_ctx, w_mod, b_mod, norm1_g, w_in, b_gate, conv_w, conv_b, w_rg, b_rg, w_ig, b_ig, lru_lambda, q_norm_g, k_norm_g, lambda_q1, lambda_k1, lambda_q2, lambda_k2, subln_g, w_branch, w_out, norm2_g, w_up, w_down, final_g):
    return _forward(x, c, ctx, c_ctx, w_mod, b_mod, norm1_g, w_in, b_gate, conv_w, conv_b, w_rg, b_rg, w_ig, b_ig,
                    lru_lambda, q_norm_g, k_norm_g, lambda_q1, lambda_k1, lambda_q2, lambda_k2, subln_g, w_branch,
                    w_out, norm2_g, w_up, w_down, final_g, tq_gqa=256, tq_diff=256, tk=2048)
```

```python
import functools
import math

import jax
import jax.numpy as jnp
from jax import lax
from jax.experimental import pallas as pl
from jax.experimental.pallas import tpu as pltpu

GRID_W = 64
HEAD_DIM = 64
LRU_WIDTH = 512
LRU_BLOCKS = 8
CONV_W = 4
CONV_LEFT = 2
LRU_C = 8.0
GQA_HEADS = 8
GQA_KV_HEADS = 2
GQA_GROUP = GQA_HEADS // GQA_KV_HEADS
DIFF_HEADS = 4
DIFF_V_DIM = 2 * HEAD_DIM
N_BRANCH = 3
ROPE_THETA = 10000.0
EPS = 1e-6
SCALE = HEAD_DIM ** -0.5

TM = 256
HALO = 8
LANES = 128
MOD_ROWS = 8
VMEM_LIMIT = 56 * 1024 * 1024

SEC_LRU = 2 * LRU_WIDTH
SEC_Q = GQA_HEADS * HEAD_DIM
SEC_KV = 2 * GQA_KV_HEADS * HEAD_DIM
SEC_DQ = DIFF_HEADS * 2 * HEAD_DIM
SEC_DV = DIFF_HEADS * DIFF_V_DIM
N_PROJ = SEC_LRU + SEC_Q + SEC_KV + 2 * SEC_DQ + SEC_DV

SAFE_LOGIT = 60.0
SAFE_VALUE = 1.0e6
BOUND_SLACK = 1.05

_BF = jnp.bfloat16
_F32 = jnp.float32


def _params(sem):
    return pltpu.CompilerParams(dimension_semantics=sem, vmem_limit_bytes=VMEM_LIMIT)


def _resident(shape, index_map):
    return pl.BlockSpec(shape, index_map, pipeline_mode=pl.Buffered(1))


def _split_bf16(a):
    hi = a.astype(_BF)
    lo = (a - hi.astype(_F32)).astype(_BF)
    return hi, lo


def _dot(a, b):
    return jnp.dot(a, b, preferred_element_type=_F32)


def _sigmoid(x):
    return 0.5 * jnp.tanh(0.5 * x) + 0.5


def _mod_kernel(c_ref, w_ref, b_ref, o_ref):
    c = c_ref[...]
    s = c * _sigmoid(c)
    s_hi, s_lo = _split_bf16(s)
    w_hi, w_lo = _split_bf16(w_ref[...])
    o_ref[...] = _dot(s_hi, w_hi) + (_dot(s_lo, w_hi) + _dot(s_hi, w_lo)) + b_ref[...]


def _modulation(cin, w_mod, b_mod):
    depth, d, n = w_mod.shape
    tn = 1536
    return pl.pallas_call(
        _mod_kernel,
        grid=(depth, n // tn),
        in_specs=[
            pl.BlockSpec((MOD_ROWS, d), lambda l, j: (0, 0)),
            pl.BlockSpec((None, d, tn), lambda l, j: (l, 0, j)),
            pl.BlockSpec((None, 1, tn), lambda l, j: (l, 0, j)),
        ],
        out_specs=pl.BlockSpec((None, MOD_ROWS, tn), lambda l, j: (l, 0, j)),
        out_shape=jax.ShapeDtypeStruct((depth, MOD_ROWS, n), _F32),
        compiler_params=_params(("arbitrary", "arbitrary")),
        name="modulation",
    )(cin, w_mod, b_mod.reshape(depth, 1, n))


def _rms_mod(x, g, scale, shift):
    y = x * lax.rsqrt(jnp.mean(x * x, axis=-1, keepdims=True) + EPS) * g
    return y * (1.0 + scale) + shift


def _swap16(y):
    lane = lax.broadcasted_iota(jnp.int32, y.shape, 1)
    return jnp.where(lane % 32 < 16, pltpu.roll(y, LANES - 16, 1), pltpu.roll(y, 16, 1))


def _rope(y, cos, sin):
    return y * cos + _swap16(y) * sin


def _chunks(z):
    return [z[:, j * LANES:(j + 1) * LANES] for j in range(z.shape[-1] // LANES)]


def _mod_row(nct, skip):
    def row(b, i, nb):
        return jnp.where(i + skip < nct, nb, b)
    return row


def _stream_specs(xs, skip, nct):
    d = xs[0].shape[-1]
    if len(xs) == 1:
        return [pl.BlockSpec((1, TM, d), lambda b, i: (b, i + skip, 0))]
    return [pl.BlockSpec((1, TM, d), lambda b, i: (b, jnp.minimum(i + skip, nct - 1), 0)),
            pl.BlockSpec((1, TM, d), lambda b, i: (b, jnp.maximum(i + skip - nct, 0), 0))]


def _stream_tile(refs, tile, nct):
    if len(refs) == 1:
        return refs[0][0]
    return jnp.where(tile < nct, refs[0][0], refs[1][0])


def _inproj_kernel(*refs, n_x, nct):
    x_refs = refs[:n_x]
    (mod_ref, g1_ref, w_ref, gq_ref, gk_ref, cos_ref, sin_ref, bd_ref,
     z0_ref, gz1_ref, qp_ref, kt_ref, vp_ref, dqp_ref, dkt_ref, dvp_ref, st_ref) = refs[n_x:]
    x = _stream_tile(x_refs, pl.program_id(1), nct)
    d = x.shape[-1]
    hn = _rms_mod(x, g1_ref[...], mod_ref[0, :, d:2 * d], mod_ref[0, :, 0:d]).astype(_BF)
    cos = cos_ref[...]
    sin = sin_ref[...]
    bd = bd_ref[...]
    lane = lax.broadcasted_iota(jnp.int32, (TM, LANES), 1)
    lo_half = lane < HEAD_DIM

    def proj(a, width):
        return _dot(hn, w_ref[:, a:a + width])

    def head_sumsq(z, exact):
        parts = []
        for h in range(z.shape[-1] // (2 * LANES)):
            sq = z[:, h * 2 * LANES:(h + 1) * 2 * LANES]
            sq = sq * sq
            if exact:
                hi, lo = _split_bf16(sq)
                parts.append(_dot(hi, bd) + _dot(lo, bd))
            else:
                parts.append(_dot(sq.astype(_BF), bd))
        return parts[0] if len(parts) == 1 else jnp.concatenate(parts, axis=-1)

    def amax(a):
        return jnp.max(jnp.max(a, axis=0, keepdims=True), axis=1, keepdims=True)

    c0 = 0
    z0_ref[0] = proj(c0, LRU_WIDTH)
    c0 += LRU_WIDTH
    gz1_ref[0] = jax.nn.gelu(proj(c0, LRU_WIDTH))
    c0 += LRU_WIDTH

    zq = proj(c0, SEC_Q)
    c0 += SEC_Q
    inv = lax.rsqrt(head_sumsq(zq, True) * (1.0 / HEAD_DIM) + EPS)
    for j, (c, r) in enumerate(zip(_chunks(zq), _chunks(inv))):
        val = _rope(c * r * gq_ref[...], cos, sin) * SCALE
        swapped = pltpu.roll(val, HEAD_DIM, 1)
        if (2 * j) // GQA_GROUP == 0:
            slots = (jnp.where(lo_half, val, 0.0), jnp.where(lo_half, swapped, 0.0))
        else:
            slots = (jnp.where(lo_half, 0.0, swapped), jnp.where(lo_half, 0.0, val))
        for n, slot in enumerate(slots):
            qp_ref[0, :, (2 * j + n) * LANES:(2 * j + n + 1) * LANES] = slot.astype(_BF)

    zkv = proj(c0, SEC_KV)
    c0 += SEC_KV
    inv = lax.rsqrt(head_sumsq(zkv, True)[:, :LANES] * (1.0 / HEAD_DIM) + EPS)
    kt_ref[0] = _rope(zkv[:, :LANES] * inv * gk_ref[...], cos, sin).T.astype(_BF)
    v = zkv[:, LANES:]
    ones_col = (lane == HEAD_DIM).astype(_F32)
    vp_ref[0, 0] = jnp.where(lo_half, v, ones_col).astype(_BF)
    vp_ref[0, 1] = jnp.where(lo_half, pltpu.roll(v, HEAD_DIM, 1), ones_col).astype(_BF)

    zdq = proj(c0, SEC_DQ)
    c0 += SEC_DQ
    for n, c in enumerate(_chunks(zdq)):
        val = _rope(c, cos, sin) * SCALE
        dqp_ref[0, :, 2 * n * LANES:(2 * n + 1) * LANES] = jnp.where(lo_half, val, 0.0).astype(_BF)
        dqp_ref[0, :, (2 * n + 1) * LANES:(2 * n + 2) * LANES] = jnp.where(lo_half, 0.0, val).astype(_BF)
    zdk = proj(c0, SEC_DQ)
    c0 += SEC_DQ
    for n, c in enumerate(_chunks(zdk)):
        dkt_ref[0, n] = _rope(c, cos, sin).T.astype(_BF)
    zdv = proj(c0, SEC_DV)
    first_lane = (lane == 0).astype(_BF)
    for n, c in enumerate(_chunks(zdv)):
        dvp_ref[0, n, :, 0:LANES] = c.astype(_BF)
        dvp_ref[0, n, :, LANES:2 * LANES] = first_lane

    stats = (amax(head_sumsq(zdq, False)) * (SCALE * SCALE), amax(head_sumsq(zdk, False)),
             amax(jnp.abs(v)), amax(jnp.abs(zdv)))
    row = lax.broadcasted_iota(jnp.int32, (8, LANES), 0)
    st = jnp.zeros((8, LANES), _F32)
    for n, val in enumerate(stats):
        st = jnp.where(row == n, val, st)
    st_ref[0, 0] = st


def _inproj(xs, mod3, g1, w_proj, gq, gk, cos, sin, bd, nct, t):
    bsz = xs[0].shape[0]
    d = xs[0].shape[-1]
    nt = t // TM
    row = _mod_row(nct, 0)
    tok = lambda w: pl.BlockSpec((1, TM, w), lambda b, i: (b, i, 0))
    const = lambda shape: pl.BlockSpec(shape, lambda b, i: (0,) * len(shape))
    out_specs = [
        tok(LRU_WIDTH), tok(LRU_WIDTH), tok(GQA_HEADS * LANES),
        pl.BlockSpec((1, LANES, TM), lambda b, i: (b, 0, i)),
        pl.BlockSpec((1, GQA_KV_HEADS, TM, LANES), lambda b, i: (b, 0, i, 0)),
        tok(2 * DIFF_HEADS * LANES),
        pl.BlockSpec((1, DIFF_HEADS, LANES, TM), lambda b, i: (b, 0, 0, i)),
        pl.BlockSpec((1, DIFF_HEADS, TM, 2 * LANES), lambda b, i: (b, 0, i, 0)),
        pl.BlockSpec((1, 1, 8, LANES), lambda b, i: (b, i, 0, 0)),
    ]
    out_shape = [
        jax.ShapeDtypeStruct((bsz, t, LRU_WIDTH), _F32), jax.ShapeDtypeStruct((bsz, t, LRU_WIDTH), _F32),
        jax.ShapeDtypeStruct((bsz, t, GQA_HEADS * LANES), _BF),
        jax.ShapeDtypeStruct((bsz, LANES, t), _BF),
        jax.ShapeDtypeStruct((bsz, GQA_KV_HEADS, t, LANES), _BF),
        jax.ShapeDtypeStruct((bsz, t, 2 * DIFF_HEADS * LANES), _BF),
        jax.ShapeDtypeStruct((bsz, DIFF_HEADS, LANES, t), _BF),
        jax.ShapeDtypeStruct((bsz, DIFF_HEADS, t, 2 * LANES), _BF),
        jax.ShapeDtypeStruct((bsz, nt, 8, LANES), _F32),
    ]
    return pl.pallas_call(
        functools.partial(_inproj_kernel, n_x=len(xs), nct=nct),
        grid=(bsz, nt),
        in_specs=_stream_specs(xs, 0, nct) + [
            pl.BlockSpec((1, 1, 2 * d), lambda b, i: (row(b, i, bsz), 0, 0)),
            const((1, d)),
            _resident((d, N_PROJ), lambda b, i: (0, 0)),
            const((1, LANES)), const((1, LANES)),
            pl.BlockSpec((TM, LANES), lambda b, i: (i, 0)),
            pl.BlockSpec((TM, LANES), lambda b, i: (i, 0)),
            const((2 * LANES, 2 * LANES)),
        ],
        out_specs=out_specs,
        out_shape=out_shape,
        compiler_params=_params(("arbitrary", "arbitrary")),
        name="inproj",
    )(*xs, mod3, g1, w_proj, gq, gk, cos, sin, bd)


def _lru_kernel(*refs, nct, nt, reverse):
    if reverse:
        (z_ref, zp_ref, zn_ref, cw_ref, cb_ref, wr_ref, br_ref, wi_ref, bi_ref, lam_ref, hf_ref, gz1_ref,
         o_ref, ext_ref, a_ref, b_ref, h_ref) = refs
    else:
        (z_ref, zp_ref, zn_ref, cw_ref, cb_ref, wr_ref, br_ref, wi_ref, bi_ref, lam_ref,
         o_ref, ext_ref, a_ref, b_ref, h_ref) = refs
    i = pl.program_id(1)
    tile = _lru_tile(i, nct, nt, reverse)

    @pl.when(i == 0)
    def _():
        h_ref[...] = jnp.zeros_like(h_ref)

    seg_first = jnp.logical_or(tile == 0, tile == nct)
    seg_last = jnp.logical_or(tile == nct - 1, tile == nt - 1)
    u = z_ref[0]
    ext_ref[0:HALO, :] = jnp.where(seg_first, 0.0, zp_ref[0])
    ext_ref[HALO:HALO + TM, :] = u
    ext_ref[HALO + TM:, :] = jnp.where(seg_last, 0.0, zn_ref[0])
    y = cb_ref[...] + cw_ref[CONV_LEFT:CONV_LEFT + 1, :] * u
    for j in range(CONV_W):
        if j != CONV_LEFT:
            y = y + cw_ref[j:j + 1, :] * ext_ref[pl.ds(HALO + j - CONV_LEFT, TM), :]

    yb = y.astype(_BF)
    r = _sigmoid(_dot(yb, wr_ref[...]) + br_ref[...])
    ig = _sigmoid(_dot(yb, wi_ref[...]) + bi_ref[...])
    log_a = r * ((-LRU_C) * jax.nn.softplus(-lam_ref[...]))
    a = jnp.exp(log_a)
    a_ref[...] = a
    b_ref[...] = jnp.sqrt(-jnp.tanh(log_a) * (a * a + 1.0)) * (ig * y)

    ng = TM // 8
    row = lax.broadcasted_iota(jnp.int32, (8, LRU_WIDTH), 0)

    def body(g, h):
        r0 = pl.multiple_of((ng - 1 - g if reverse else g) * 8, 8)
        aa = a_ref[pl.ds(r0, 8), :]
        bb = b_ref[pl.ds(r0, 8), :]
        for s in (1, 2, 4):
            sh = 8 - s if reverse else s
            m = (row < 8 - s) if reverse else (row >= s)
            bb = jnp.where(m, aa * pltpu.roll(bb, sh, 0) + bb, bb)
            aa = jnp.where(m, aa * pltpu.roll(aa, sh, 0), aa)
        hh = aa * h + bb
        if reverse:
            hsum = hf_ref[0, pl.ds(r0, 8), :] + hh
            o_ref[0, pl.ds(r0, 8), :] = (hsum * gz1_ref[0, pl.ds(r0, 8), :]).astype(o_ref.dtype)
            return hh[0:1, :]
        o_ref[0, pl.ds(r0, 8), :] = hh
        return hh[7:8, :]

    h_ref[0:1, :] = lax.fori_loop(0, ng, body, h_ref[0:1, :], unroll=2)


def _lru_tile(i, nct, nt, reverse):
    if not reverse:
        return i
    return jnp.where(i < nct, nct - 1 - i, nt - 1 - (i - nct))


def _lru_sweep(z0, conv_w, conv_b, w_r, b_r, w_i, b_i, lam, nct, reverse, hf=None, gz1=None):
    bsz, t, w = z0.shape
    nt = t // TM
    hb = TM // HALO
    tile = lambda i: _lru_tile(i, nct, nt, reverse)
    tok = pl.BlockSpec((1, TM, w), lambda b, i: (b, tile(i), 0))
    const = lambda shape: pl.BlockSpec(shape, lambda b, i: (0,) * len(shape))
    in_specs = [
        tok,
        pl.BlockSpec((1, HALO, w), lambda b, i: (b, jnp.maximum(tile(i) * hb - 1, 0), 0)),
        pl.BlockSpec((1, HALO, w), lambda b, i: (b, jnp.minimum((tile(i) + 1) * hb, nt * hb - 1), 0)),
        const((CONV_W, w)), const((1, w)), const((w, w)), const((1, w)), const((w, w)), const((1, w)), const((1, w)),
    ]
    args = [z0, z0, z0, conv_w, conv_b, w_r, b_r, w_i, b_i, lam]
    if reverse:
        in_specs += [tok, tok]
        args += [hf, gz1]
    return pl.pallas_call(
        functools.partial(_lru_kernel, nct=nct, nt=nt, reverse=reverse),
        grid=(bsz, nt),
        in_specs=in_specs,
        out_specs=tok,
        out_shape=jax.ShapeDtypeStruct((bsz, t, w), _BF if reverse else _F32),
        scratch_shapes=[pltpu.VMEM((TM + 2 * HALO, w), _F32), pltpu.VMEM((TM, w), _F32),
                        pltpu.VMEM((TM, w), _F32), pltpu.VMEM((8, w), _F32)],
        compiler_params=_params(("arbitrary", "arbitrary")),
        name="lru_bwd" if reverse else "lru_fwd",
    )(*args)


def _attend(pairs, v_at, v_width, is_latent, n_ctx, n_lat, tk, stabilize):
    rows = pairs[0][0].shape[0]

    def step(carry, start, size):
        v = v_at(start, size)
        out = []
        for (q, kt_at), state in zip(pairs, carry):
            s = _dot(q, kt_at(start, size))
            if stabilize:
                m, acc = state
                m_new = jnp.maximum(m, jnp.max(s, axis=-1, keepdims=True))
                acc = jnp.exp(m - m_new) * acc + _dot(jnp.exp(s - m_new).astype(_BF), v)
                out.append((m_new, acc))
            else:
                out.append((state[0] + _dot(jnp.exp(s).astype(_BF), v),))
        return tuple(out)

    zero = jnp.zeros((rows, v_width), _F32)
    init = (jnp.full((rows, 1), -jnp.inf, _F32), zero) if stabilize else (zero,)
    init = (init,) * len(pairs)

    def all_keys():
        carry = step(init, 0, n_ctx + tk)
        return lax.fori_loop(
            1, n_lat // tk, lambda j, c: step(c, pl.multiple_of(n_ctx + j * tk, math.gcd(n_ctx, tk)), tk), carry)

    carry = lax.cond(is_latent, all_keys, lambda: step(init, 0, n_ctx))
    return [state[-1] for state in carry]


def _attend_guarded(bounded, pairs, v_at, v_width, is_latent, n_ctx, n_lat, tk):
    return lax.cond(bounded,
                    lambda: _attend(pairs, v_at, v_width, is_latent, n_ctx, n_lat, n_lat, False),
                    lambda: _attend(pairs, v_at, v_width, is_latent, n_ctx, n_lat, tk, True))


def _gqa_kernel(flag_ref, q_ref, kt_ref, v_ref, o_ref, *, nct, tk):
    tq = q_ref.shape[1]
    n_ctx = nct * TM
    q = jnp.concatenate(_chunks(q_ref[0]), axis=0)
    (acc,) = _attend_guarded(
        flag_ref[0] != 0, [(q, lambda a, n: kt_ref[0, :, pl.ds(a, n)])], lambda a, n: v_ref[0, 0, pl.ds(a, n), :],
        v_ref.shape[-1], pl.program_id(2) >= n_ctx // tq, n_ctx, kt_ref.shape[-1] - n_ctx, tk)
    o = acc / acc[:, HEAD_DIM:HEAD_DIM + 1]
    lo_half = lax.broadcasted_iota(jnp.int32, (tq, LANES), 1) < HEAD_DIM
    pairs = [jnp.where(lo_half, o[2 * p * tq:(2 * p + 1) * tq], pltpu.roll(o[(2 * p + 1) * tq:(2 * p + 2) * tq], HEAD_DIM, 1))
             for p in range(GQA_GROUP // 2)]
    o_ref[0] = jnp.concatenate(pairs, axis=-1).astype(o_ref.dtype)


def _gqa_attention(flag, qp, kt, vp, *, nct, tq, tk):
    bsz, t, _ = qp.shape
    gw = GQA_GROUP * LANES
    return pl.pallas_call(
        functools.partial(_gqa_kernel, nct=nct, tk=tk),
        grid_spec=pltpu.PrefetchScalarGridSpec(
            num_scalar_prefetch=1,
            grid=(bsz, GQA_KV_HEADS, t // tq),
            in_specs=[
                pl.BlockSpec((1, tq, gw), lambda b, g, i, f: (b, i, g)),
                pl.BlockSpec((1, LANES, t), lambda b, g, i, f: (b, 0, 0)),
                pl.BlockSpec((1, 1, t, LANES), lambda b, g, i, f: (b, g, 0, 0)),
            ],
            out_specs=pl.BlockSpec((1, tq, GQA_GROUP * HEAD_DIM), lambda b, g, i, f: (b, i, g)),
        ),
        out_shape=jax.ShapeDtypeStruct((bsz, t, GQA_HEADS * HEAD_DIM), _BF),
        compiler_params=_params(("arbitrary", "arbitrary", "arbitrary")),
        name="gqa_attention",
    )(flag, qp, kt, vp)


def _diff_kernel(flag_ref, q_ref, kt_ref, v_ref, lq1_ref, lk1_ref, lq2_ref, lk2_ref, g_ref, o_ref, *,
                 nct, tk, lam_init):
    tq = q_ref.shape[1]
    n_ctx = nct * TM
    kt_at = lambda a, n: kt_ref[0, 0, :, pl.ds(a, n)]
    acc1, acc2 = _attend_guarded(
        flag_ref[0] != 0, [(q, kt_at) for q in _chunks(q_ref[0])], lambda a, n: v_ref[0, 0, pl.ds(a, n), :],
        v_ref.shape[-1], pl.program_id(2) >= n_ctx // tq, n_ctx, kt_ref.shape[-1] - n_ctx, tk)
    lam = (jnp.exp(jnp.sum(lq1_ref[...] * lk1_ref[...], axis=-1, keepdims=True))
           - jnp.exp(jnp.sum(lq2_ref[...] * lk2_ref[...], axis=-1, keepdims=True)) + lam_init)
    o = (acc1[:, :DIFF_V_DIM] / acc1[:, DIFF_V_DIM:DIFF_V_DIM + 1]
         - lam * (acc2[:, :DIFF_V_DIM] / acc2[:, DIFF_V_DIM:DIFF_V_DIM + 1]))
    o = o * lax.rsqrt(jnp.mean(o * o, axis=-1, keepdims=True) + EPS) * g_ref[...]
    o_ref[0] = (o * (1.0 - lam_init)).astype(o_ref.dtype)


def _diff_attention(flag, dqp, dkt, dvp, lq1, lk1, lq2, lk2, subln_g, *, nct, tq, tk, lam_init):
    bsz, t, _ = dqp.shape
    vec = pl.BlockSpec((1, HEAD_DIM), lambda b, n, i, f: (0, 0))
    return pl.pallas_call(
        functools.partial(_diff_kernel, nct=nct, tk=tk, lam_init=lam_init),
        grid_spec=pltpu.PrefetchScalarGridSpec(
            num_scalar_prefetch=1,
            grid=(bsz, DIFF_HEADS, t // tq),
            in_specs=[
                pl.BlockSpec((1, tq, 2 * LANES), lambda b, n, i, f: (b, i, n)),
                pl.BlockSpec((1, 1, LANES, t), lambda b, n, i, f: (b, n, 0, 0)),
                pl.BlockSpec((1, 1, t, 2 * LANES), lambda b, n, i, f: (b, n, 0, 0)),
                vec, vec, vec, vec,
                pl.BlockSpec((1, DIFF_V_DIM), lambda b, n, i, f: (0, 0)),
            ],
            out_specs=pl.BlockSpec((1, tq, DIFF_V_DIM), lambda b, n, i, f: (b, i, n)),
        ),
        out_shape=jax.ShapeDtypeStruct((bsz, t, DIFF_HEADS * DIFF_V_DIM), _BF),
        compiler_params=_params(("arbitrary", "arbitrary", "arbitrary")),
        name="diff_attention",
    )(flag, dqp, dkt, dvp, lq1, lk1, lq2, lk2, subln_g)


def _merge_kernel(*refs, n_x, nct, skip):
    x_refs = refs[:n_x]
    mod_ref, g1_ref, wg_ref, bg_ref, yr_ref, ya_ref, yd_ref, wb_ref, wo_ref, o_ref = refs[n_x:]
    x = _stream_tile(x_refs, pl.program_id(1) + skip, nct)
    d = x.shape[-1]
    hn = _rms_mod(x, g1_ref[...], mod_ref[0, :, d:2 * d], mod_ref[0, :, 0:d]).astype(_BF)
    gate = mod_ref[0, :, 2 * d:3 * d]
    ys = (yr_ref[0], ya_ref[0], yd_ref[0])
    m = None
    for n in range(N_BRANCH):
        g = _sigmoid(_dot(hn, wg_ref[:, n * d:(n + 1) * d]) + bg_ref[:, n * d:(n + 1) * d])
        term = g * _dot(ys[n], wb_ref[n])
        m = term if m is None else m + term
    o_ref[0] = x + gate * _dot(m.astype(_BF), wo_ref[...])


def _merge(xs, mod3, g1, w_gate, b_gate, y_rec, y_gqa, y_diff, w_branch, w_out, nct, skip):
    bsz, t, bw = y_rec.shape
    d = xs[0].shape[-1]
    nt = t // TM - skip
    row = _mod_row(nct, skip)
    tok = lambda w: pl.BlockSpec((1, TM, w), lambda b, i: (b, i + skip, 0))
    return pl.pallas_call(
        functools.partial(_merge_kernel, n_x=len(xs), nct=nct, skip=skip),
        grid=(bsz, nt),
        in_specs=_stream_specs(xs, skip, nct) + [
            pl.BlockSpec((1, 1, 3 * d), lambda b, i: (row(b, i, bsz), 0, 0)),
            pl.BlockSpec((1, d), lambda b, i: (0, 0)),
            _resident((d, N_BRANCH * d), lambda b, i: (0, 0)),
            pl.BlockSpec((1, N_BRANCH * d), lambda b, i: (0, 0)),
            tok(bw), tok(bw), tok(bw),
            _resident((N_BRANCH, bw, d), lambda b, i: (0, 0, 0)),
            _resident((d, d), lambda b, i: (0, 0)),
        ],
        out_specs=pl.BlockSpec((1, TM, d), lambda b, i: (b, i, 0)),
        out_shape=jax.ShapeDtypeStruct((bsz, nt * TM, d), _F32),
        compiler_params=_params(("arbitrary", "arbitrary")),
        name="merge",
    )(*xs, mod3, g1, w_gate, b_gate, y_rec, y_gqa, y_diff, w_branch, w_out)


def _mlp_kernel(x_ref, mod_ref, g2_ref, wu_ref, wd_ref, gf_ref, o_ref, *, final):
    d = x_ref.shape[-1]
    x = x_ref[0]
    h = _rms_mod(x, g2_ref[...], mod_ref[0, :, d:2 * d], mod_ref[0, :, 0:d]).astype(_BF)
    gate = mod_ref[0, :, 2 * d:3 * d]
    d_ff = wu_ref.shape[-1]
    acc = None
    for c in range(d_ff // d):
        u = jnp.maximum(_dot(h, wu_ref[:, c * d:(c + 1) * d]), 0.0)
        part = _dot((u * u).astype(_BF), wd_ref[c * d:(c + 1) * d, :])
        acc = part if acc is None else acc + part
    y = x + gate * acc
    if final:
        y = y * lax.rsqrt(jnp.mean(y * y, axis=-1, keepdims=True) + EPS) * gf_ref[...]
    o_ref[0] = y


def _mlp(xm, mod3, g2, w_up, w_down, final_g, nct, skip, final):
    bsz, t, d = xm.shape
    d_ff = w_up.shape[-1]
    row = _mod_row(nct, skip)
    tok = pl.BlockSpec((1, TM, d), lambda b, i: (b, i, 0))
    return pl.pallas_call(
        functools.partial(_mlp_kernel, final=final),
        grid=(bsz, t // TM),
        in_specs=[
            tok,
            pl.BlockSpec((1, 1, 3 * d), lambda b, i: (row(b, i, bsz), 0, 1)),
            pl.BlockSpec((1, d), lambda b, i: (0, 0)),
            _resident((d, d_ff), lambda b, i: (0, 0)),
            _resident((d_ff, d), lambda b, i: (0, 0)),
            pl.BlockSpec((1, d), lambda b, i: (0, 0)),
        ],
        out_specs=tok,
        out_shape=jax.ShapeDtypeStruct(xm.shape, _F32),
        compiler_params=_params(("arbitrary", "arbitrary")),
        name="mlp",
    )(xm, mod3, g2, w_up, w_down, final_g)


def _rope_tables(n_ctx, s):
    rows = s // GRID_W
    pos_r = jnp.repeat(jnp.arange(rows, dtype=_F32), GRID_W)
    pos_c = jnp.tile(jnp.arange(GRID_W, dtype=_F32), rows)
    n_freq = HEAD_DIM // 4
    inv = ROPE_THETA ** (-jnp.arange(n_freq, dtype=_F32) * 2.0 / (HEAD_DIM // 2))
    ang_r = pos_r[:, None] * inv
    ang_c = pos_c[:, None] * inv
    ang = jnp.concatenate([ang_r, ang_r, ang_c, ang_c], axis=-1)
    sign = jnp.tile(jnp.concatenate([-jnp.ones(n_freq, _F32), jnp.ones(n_freq, _F32)]), 2)
    cos = jnp.concatenate([jnp.ones((n_ctx, HEAD_DIM), _F32), jnp.cos(ang)], axis=0)
    sin = jnp.concatenate([jnp.zeros((n_ctx, HEAD_DIM), _F32), jnp.sin(ang) * sign], axis=0)
    return jnp.tile(cos, (1, LANES // HEAD_DIM)), jnp.tile(sin, (1, LANES // HEAD_DIM))


def _block_diag(w):
    nb, n, _ = w.shape
    eye = jnp.eye(nb, dtype=w.dtype)
    return jnp.einsum('ncd,nm->ncmd', w, eye).reshape(nb * n, nb * n)


def _forward(x, c, ctx, c_ctx, w_mod, b_mod, norm1_g, w_in, b_gate, conv_w, conv_b, w_rg, b_rg, w_ig, b_ig,
             lru_lambda, q_norm_g, k_norm_g, lambda_q1, lambda_k1, lambda_q2, lambda_k2, subln_g, w_branch,
             w_out, norm2_g, w_up, w_down, final_g, *, tq_gqa, tq_diff, tk):
    bsz, s, d = x.shape
    n_ctx = ctx.shape[1]
    depth = w_mod.shape[0]
    assert n_ctx % TM == 0 and n_ctx > 0 and s % TM == 0 and s % tk == 0 and bsz < MOD_ROWS
    assert TM % tq_gqa == 0 and TM % tq_diff == 0
    nct = n_ctx // TM
    t = n_ctx + s

    cin = jnp.zeros((MOD_ROWS, d), _F32).at[:bsz].set(c).at[bsz].set(c_ctx)
    mod = _modulation(cin, w_mod, b_mod)
    cos, sin = _rope_tables(n_ctx, s)
    bd = jnp.kron(jnp.eye(2 * LANES // HEAD_DIM, dtype=_F32), jnp.ones((HEAD_DIM, HEAD_DIM), _F32)).astype(_BF)
    xs = (ctx, x)

    for l in range(depth):
        last = l == depth - 1
        skip = nct if last else 0
        lam_init = 0.8 - 0.6 * math.exp(-0.3 * l)
        mod3 = mod[l].reshape(MOD_ROWS, 1, 6 * d)
        w_proj = w_in[l, :, :N_PROJ].astype(_BF)
        w_gate = w_in[l, :, N_PROJ:].astype(_BF)
        gq = jnp.tile(q_norm_g[l], LANES // HEAD_DIM).reshape(1, LANES)
        gk = jnp.tile(k_norm_g[l], LANES // HEAD_DIM).reshape(1, LANES)
        z0, gz1, qp, kt, vp, dqp, dkt, dvp, stats = _inproj(
            xs, mod3, norm1_g[l].reshape(1, d), w_proj, gq, gk, cos, sin, bd, nct, t)

        lru = lambda dr, rev, **kw: _lru_sweep(
            z0, conv_w[l], conv_b[l].reshape(1, -1), _block_diag(w_rg[l, dr]).astype(_BF), b_rg[l, dr].reshape(1, -1),
            _block_diag(w_ig[l, dr]).astype(_BF), b_ig[l, dr].reshape(1, -1), lru_lambda[l, dr].reshape(1, -1),
            nct, rev, **kw)
        hf = lru(0, False)
        y_rec = lru(1, True, hf=hf, gz1=gz1)

        st = jnp.max(stats, axis=(0, 1, 3))
        gqa_bound = math.sqrt(HEAD_DIM) * jnp.max(jnp.abs(q_norm_g[l])) * jnp.max(jnp.abs(k_norm_g[l]))
        gqa_ok = jnp.logical_and(gqa_bound * BOUND_SLACK <= SAFE_LOGIT, st[2] <= SAFE_VALUE)
        diff_ok = jnp.logical_and(st[0] * st[1] * BOUND_SLACK <= SAFE_LOGIT * SAFE_LOGIT, st[3] <= SAFE_VALUE)
        y_gqa = _gqa_attention(gqa_ok.astype(jnp.int32).reshape(1), qp, kt, vp, nct=nct, tq=tq_gqa, tk=tk)
        vec = lambda a: a[l].reshape(1, HEAD_DIM)
        y_diff = _diff_attention(diff_ok.astype(jnp.int32).reshape(1), dqp, dkt, dvp, vec(lambda_q1), vec(lambda_k1),
                                 vec(lambda_q2), vec(lambda_k2), subln_g[l].reshape(1, DIFF_V_DIM),
                                 nct=nct, tq=tq_diff, tk=tk, lam_init=lam_init)

        xm = _merge(xs, mod3, norm1_g[l].reshape(1, d), w_gate, b_gate[l].reshape(1, -1), y_rec, y_gqa, y_diff,
                    w_branch[l].astype(_BF), w_out[l].astype(_BF), nct, skip)
        xs = (_mlp(xm, mod3, norm2_g[l].reshape(1, d), w_up[l].astype(_BF), w_down[l].astype(_BF),
                   final_g.reshape(1, d), nct, skip, last),)
    return xs[0]


def kernel(x, c, ctx, c_ctx, w_mod, b_mod, norm1_g, w_in, b_gate, conv_w, conv_b, w_rg, b_rg, w_ig, b_ig, lru_lambda, q_norm_g, k_norm_g, lambda_q1, lambda_k1, lambda_q2, lambda_k2, subln_g, w_branch, w_out, norm2_g, w_up, w_down, final_g):
    return _forward(x, c, ctx, c_ctx, w_mod, b_mod, norm1_g, w_in, b_gate, conv_w, conv_b, w_rg, b_rg, w_ig, b_ig,
                    lru_lambda, q_norm_g, k_norm_g, lambda_q1, lambda_k1, lambda_q2, lambda_k2, subln_g, w_branch,
                    w_out, norm2_g, w_up, w_down, final_g, tq_gqa=256, tq_diff=256, tk=2048)
```

```python
import functools
import math

import jax
import jax.numpy as jnp
from jax import lax
from jax.experimental import pallas as pl
from jax.experimental.pallas import tpu as pltpu

GRID_W = 64
HEAD_DIM = 64
LRU_WIDTH = 512
LRU_BLOCKS = 8
CONV_W = 4
CONV_LEFT = 2
LRU_C = 8.0
GQA_HEADS = 8
GQA_KV_HEADS = 2
GQA_GROUP = GQA_HEADS // GQA_KV_HEADS
DIFF_HEADS = 4
DIFF_V_DIM = 2 * HEAD_DIM
N_BRANCH = 3
ROPE_THETA = 10000.0
EPS = 1e-6
SCALE = HEAD_DIM ** -0.5

TM = 256
HALO = 8
LANES = 128
MOD_ROWS = 8
ONES_ROWS = 16
VMEM_LIMIT = 56 * 1024 * 1024

SEC_LRU = 2 * LRU_WIDTH
SEC_Q = GQA_HEADS * HEAD_DIM
SEC_KV = 2 * GQA_KV_HEADS * HEAD_DIM
SEC_DQ = DIFF_HEADS * 2 * HEAD_DIM
SEC_DV = DIFF_HEADS * DIFF_V_DIM
N_PROJ = SEC_LRU + SEC_Q + SEC_KV + 2 * SEC_DQ + SEC_DV

SAFE_LOGIT = 60.0
SAFE_VALUE = 1.0e6
BOUND_SLACK = 1.05

_BF = jnp.bfloat16
_F32 = jnp.float32


def _params(sem):
    return pltpu.CompilerParams(dimension_semantics=sem, vmem_limit_bytes=VMEM_LIMIT)


def _resident(shape, index_map):
    return pl.BlockSpec(shape, index_map, pipeline_mode=pl.Buffered(1))


def _split_bf16(a):
    hi = a.astype(_BF)
    lo = (a - hi.astype(_F32)).astype(_BF)
    return hi, lo


def _dot(a, b):
    return jnp.dot(a, b, preferred_element_type=_F32)


def _sigmoid(x):
    return 0.5 * jnp.tanh(0.5 * x) + 0.5


def _mod_kernel(c_ref, w_ref, b_ref, o_ref):
    c = c_ref[...]
    s = c * _sigmoid(c)
    s_hi, s_lo = _split_bf16(s)
    w_hi, w_lo = _split_bf16(w_ref[...])
    o_ref[...] = _dot(s_hi, w_hi) + (_dot(s_lo, w_hi) + _dot(s_hi, w_lo)) + b_ref[...]


def _modulation(cin, w_mod, b_mod):
    depth, d, n = w_mod.shape
    tn = 1536
    return pl.pallas_call(
        _mod_kernel,
        grid=(depth, n // tn),
        in_specs=[
            pl.BlockSpec((MOD_ROWS, d), lambda l, j: (0, 0)),
            pl.BlockSpec((None, d, tn), lambda l, j: (l, 0, j)),
            pl.BlockSpec((None, 1, tn), lambda l, j: (l, 0, j)),
        ],
        out_specs=pl.BlockSpec((None, MOD_ROWS, tn), lambda l, j: (l, 0, j)),
        out_shape=jax.ShapeDtypeStruct((depth, MOD_ROWS, n), _F32),
        compiler_params=_params(("arbitrary", "arbitrary")),
        name="modulation",
    )(cin, w_mod, b_mod.reshape(depth, 1, n))


def _rms_mod(x, g, scale, shift):
    y = x * lax.rsqrt(jnp.mean(x * x, axis=-1, keepdims=True) + EPS) * g
    return y * (1.0 + scale) + shift


def _swap16(y):
    lane = lax.broadcasted_iota(jnp.int32, y.shape, 1)
    return jnp.where(lane % 32 < 16, pltpu.roll(y, LANES - 16, 1), pltpu.roll(y, 16, 1))


def _rope(y, cos, sin):
    return y * cos + _swap16(y) * sin


def _chunks(z):
    return [z[:, j * LANES:(j + 1) * LANES] for j in range(z.shape[-1] // LANES)]


def _mod_row(nct, skip):
    def row(b, i, nb):
        return jnp.where(i + skip < nct, nb, b)
    return row


def _stream_specs(xs, skip, nct):
    d = xs[0].shape[-1]
    if len(xs) == 1:
        return [pl.BlockSpec((1, TM, d), lambda b, i: (b, i + skip, 0))]
    return [pl.BlockSpec((1, TM, d), lambda b, i: (b, jnp.minimum(i + skip, nct - 1), 0)),
            pl.BlockSpec((1, TM, d), lambda b, i: (b, jnp.maximum(i + skip - nct, 0), 0))]


def _stream_tile(refs, tile, nct):
    if len(refs) == 1:
        return refs[0][0]
    return jnp.where(tile < nct, refs[0][0], refs[1][0])


def _inproj_kernel(*refs, n_x, nct):
    x_refs = refs[:n_x]
    (mod_ref, g1_ref, w_ref, gq_ref, gk_ref, cos_ref, sin_ref, bd_ref,
     z0_ref, gz1_ref, qp_ref, kn_ref, vt_ref, dqp_ref, dkn_ref, dvt_ref, st_ref) = refs[n_x:]
    x = _stream_tile(x_refs, pl.program_id(1), nct)
    d = x.shape[-1]
    hn = _rms_mod(x, g1_ref[...], mod_ref[0, :, d:2 * d], mod_ref[0, :, 0:d]).astype(_BF)
    cos = cos_ref[...]
    sin = sin_ref[...]
    bd = bd_ref[...]
    lane = lax.broadcasted_iota(jnp.int32, (TM, LANES), 1)
    lo_half = lane < HEAD_DIM

    def proj(a, width):
        return _dot(hn, w_ref[:, a:a + width])

    def head_sumsq(z, exact):
        parts = []
        for h in range(z.shape[-1] // (2 * LANES)):
            sq = z[:, h * 2 * LANES:(h + 1) * 2 * LANES]
            sq = sq * sq
            if exact:
                hi, lo = _split_bf16(sq)
                parts.append(_dot(hi, bd) + _dot(lo, bd))
            else:
                parts.append(_dot(sq.astype(_BF), bd))
        return parts[0] if len(parts) == 1 else jnp.concatenate(parts, axis=-1)

    def amax(a):
        return jnp.max(jnp.max(a, axis=0, keepdims=True), axis=1, keepdims=True)

    c0 = 0
    z0_ref[0] = proj(c0, LRU_WIDTH)
    c0 += LRU_WIDTH
    gz1_ref[0] = jax.nn.gelu(proj(c0, LRU_WIDTH))
    c0 += LRU_WIDTH

    zq = proj(c0, SEC_Q)
    c0 += SEC_Q
    inv = lax.rsqrt(head_sumsq(zq, True) * (1.0 / HEAD_DIM) + EPS)
    for j, (c, r) in enumerate(zip(_chunks(zq), _chunks(inv))):
        val = _rope(c * r * gq_ref[...], cos, sin) * SCALE
        swapped = pltpu.roll(val, HEAD_DIM, 1)
        if (2 * j) // GQA_GROUP == 0:
            slots = (jnp.where(lo_half, val, 0.0), jnp.where(lo_half, swapped, 0.0))
        else:
            slots = (jnp.where(lo_half, 0.0, swapped), jnp.where(lo_half, 0.0, val))
        for n, slot in enumerate(slots):
            qp_ref[0, :, (2 * j + n) * LANES:(2 * j + n + 1) * LANES] = slot.astype(_BF)

    zkv = proj(c0, SEC_KV)
    c0 += SEC_KV
    inv = lax.rsqrt(head_sumsq(zkv, True)[:, :LANES] * (1.0 / HEAD_DIM) + EPS)
    kn_ref[0] = _rope(zkv[:, :LANES] * inv * gk_ref[...], cos, sin).astype(_BF)
    v = zkv[:, LANES:]
    ones_rows = (lax.broadcasted_iota(jnp.int32, (ONES_ROWS, TM), 0) == 0).astype(_BF)
    vt = v.T.astype(_BF)
    for g in range(GQA_KV_HEADS):
        vt_ref[0, g, 0:HEAD_DIM] = vt[g * HEAD_DIM:(g + 1) * HEAD_DIM]
        vt_ref[0, g, HEAD_DIM:HEAD_DIM + ONES_ROWS] = ones_rows

    zdq = proj(c0, SEC_DQ)
    c0 += SEC_DQ
    for n, c in enumerate(_chunks(zdq)):
        val = _rope(c, cos, sin) * SCALE
        dqp_ref[0, :, 2 * n * LANES:(2 * n + 1) * LANES] = jnp.where(lo_half, val, 0.0).astype(_BF)
        dqp_ref[0, :, (2 * n + 1) * LANES:(2 * n + 2) * LANES] = jnp.where(lo_half, 0.0, val).astype(_BF)
    zdk = proj(c0, SEC_DQ)
    c0 += SEC_DQ
    for n, c in enumerate(_chunks(zdk)):
        dkn_ref[0, :, n * LANES:(n + 1) * LANES] = _rope(c, cos, sin).astype(_BF)
    zdv = proj(c0, SEC_DV)
    for n, c in enumerate(_chunks(zdv)):
        dvt_ref[0, n, 0:DIFF_V_DIM] = c.T.astype(_BF)
        dvt_ref[0, n, DIFF_V_DIM:DIFF_V_DIM + ONES_ROWS] = ones_rows

    stats = (amax(head_sumsq(zdq, False)) * (SCALE * SCALE), amax(head_sumsq(zdk, False)),
             amax(jnp.abs(v)), amax(jnp.abs(zdv)))
    row = lax.broadcasted_iota(jnp.int32, (8, LANES), 0)
    st = jnp.zeros((8, LANES), _F32)
    for n, val in enumerate(stats):
        st = jnp.where(row == n, val, st)
    st_ref[0, 0] = st


def _inproj(xs, mod3, g1, w_proj, gq, gk, cos, sin, bd, nct, t):
    bsz = xs[0].shape[0]
    d = xs[0].shape[-1]
    nt = t // TM
    row = _mod_row(nct, 0)
    tok = lambda w: pl.BlockSpec((1, TM, w), lambda b, i: (b, i, 0))
    const = lambda shape: pl.BlockSpec(shape, lambda b, i: (0,) * len(shape))
    out_specs = [
        tok(LRU_WIDTH), tok(LRU_WIDTH), tok(GQA_HEADS * LANES),
        tok(LANES),
        pl.BlockSpec((1, GQA_KV_HEADS, HEAD_DIM + ONES_ROWS, TM), lambda b, i: (b, 0, 0, i)),
        tok(2 * DIFF_HEADS * LANES),
        tok(DIFF_HEADS * LANES),
        pl.BlockSpec((1, DIFF_HEADS, DIFF_V_DIM + ONES_ROWS, TM), lambda b, i: (b, 0, 0, i)),
        pl.BlockSpec((1, 1, 8, LANES), lambda b, i: (b, i, 0, 0)),
    ]
    out_shape = [
        jax.ShapeDtypeStruct((bsz, t, LRU_WIDTH), _F32), jax.ShapeDtypeStruct((bsz, t, LRU_WIDTH), _F32),
        jax.ShapeDtypeStruct((bsz, t, GQA_HEADS * LANES), _BF),
        jax.ShapeDtypeStruct((bsz, t, LANES), _BF),
        jax.ShapeDtypeStruct((bsz, GQA_KV_HEADS, HEAD_DIM + ONES_ROWS, t), _BF),
        jax.ShapeDtypeStruct((bsz, t, 2 * DIFF_HEADS * LANES), _BF),
        jax.ShapeDtypeStruct((bsz, t, DIFF_HEADS * LANES), _BF),
        jax.ShapeDtypeStruct((bsz, DIFF_HEADS, DIFF_V_DIM + ONES_ROWS, t), _BF),
        jax.ShapeDtypeStruct((bsz, nt, 8, LANES), _F32),
    ]
    return pl.pallas_call(
        functools.partial(_inproj_kernel, n_x=len(xs), nct=nct),
        grid=(bsz, nt),
        in_specs=_stream_specs(xs, 0, nct) + [
            pl.BlockSpec((1, 1, 2 * d), lambda b, i: (row(b, i, bsz), 0, 0)),
            const((1, d)),
            _resident((d, N_PROJ), lambda b, i: (0, 0)),
            const((1, LANES)), const((1, LANES)),
            pl.BlockSpec((TM, LANES), lambda b, i: (i, 0)),
            pl.BlockSpec((TM, LANES), lambda b, i: (i, 0)),
            const((2 * LANES, 2 * LANES)),
        ],
        out_specs=out_specs,
        out_shape=out_shape,
        compiler_params=_params(("arbitrary", "arbitrary")),
        name="inproj",
    )(*xs, mod3, g1, w_proj, gq, gk, cos, sin, bd)


def _lru_kernel(*refs, nct, nt, reverse):
    if reverse:
        (z_ref, zp_ref, zn_ref, cw_ref, cb_ref, wr_ref, br_ref, wi_ref, bi_ref, lam_ref, hf_ref, gz1_ref,
         o_ref, ext_ref, a_ref, b_ref, h_ref) = refs
    else:
        (z_ref, zp_ref, zn_ref, cw_ref, cb_ref, wr_ref, br_ref, wi_ref, bi_ref, lam_ref,
         o_ref, ext_ref, a_ref, b_ref, h_ref) = refs
    i = pl.program_id(1)
    tile = _lru_tile(i, nct, nt, reverse)

    @pl.when(i == 0)
    def _():
        h_ref[...] = jnp.zeros_like(h_ref)

    seg_first = jnp.logical_or(tile == 0, tile == nct)
    seg_last = jnp.logical_or(tile == nct - 1, tile == nt - 1)
    u = z_ref[0]
    ext_ref[0:HALO, :] = jnp.where(seg_first, 0.0, zp_ref[0])
    ext_ref[HALO:HALO + TM, :] = u
    ext_ref[HALO + TM:, :] = jnp.where(seg_last, 0.0, zn_ref[0])
    y = cb_ref[...] + cw_ref[CONV_LEFT:CONV_LEFT + 1, :] * u
    for j in range(CONV_W):
        if j != CONV_LEFT:
            y = y + cw_ref[j:j + 1, :] * ext_ref[pl.ds(HALO + j - CONV_LEFT, TM), :]

    yb = y.astype(_BF)
    r = _sigmoid(_dot(yb, wr_ref[...]) + br_ref[...])
    ig = _sigmoid(_dot(yb, wi_ref[...]) + bi_ref[...])
    log_a = r * ((-LRU_C) * jax.nn.softplus(-lam_ref[...]))
    a = jnp.exp(log_a)
    a_ref[...] = a
    b_ref[...] = jnp.sqrt(-jnp.tanh(log_a) * (a * a + 1.0)) * (ig * y)

    ng = TM // 8
    row = lax.broadcasted_iota(jnp.int32, (8, LRU_WIDTH), 0)

    def body(g, h):
        r0 = pl.multiple_of((ng - 1 - g if reverse else g) * 8, 8)
        aa = a_ref[pl.ds(r0, 8), :]
        bb = b_ref[pl.ds(r0, 8), :]
        for s in (1, 2, 4):
            sh = 8 - s if reverse else s
            m = (row < 8 - s) if reverse else (row >= s)
            bb = jnp.where(m, aa * pltpu.roll(bb, sh, 0) + bb, bb)
            aa = jnp.where(m, aa * pltpu.roll(aa, sh, 0), aa)
        hh = aa * h + bb
        if reverse:
            hsum = hf_ref[0, pl.ds(r0, 8), :] + hh
            o_ref[0, pl.ds(r0, 8), :] = (hsum * gz1_ref[0, pl.ds(r0, 8), :]).astype(o_ref.dtype)
            return hh[0:1, :]
        o_ref[0, pl.ds(r0, 8), :] = hh
        return hh[7:8, :]

    h_ref[0:1, :] = lax.fori_loop(0, ng, body, h_ref[0:1, :], unroll=2)


def _lru_tile(i, nct, nt, reverse):
    if not reverse:
        return i
    return jnp.where(i < nct, nct - 1 - i, nt - 1 - (i - nct))


def _lru_sweep(z0, conv_w, conv_b, w_r, b_r, w_i, b_i, lam, nct, reverse, hf=None, gz1=None):
    bsz, t, w = z0.shape
    nt = t // TM
    hb = TM // HALO
    tile = lambda i: _lru_tile(i, nct, nt, reverse)
    tok = pl.BlockSpec((1, TM, w), lambda b, i: (b, tile(i), 0))
    const = lambda shape: pl.BlockSpec(shape, lambda b, i: (0,) * len(shape))
    in_specs = [
        tok,
        pl.BlockSpec((1, HALO, w), lambda b, i: (b, jnp.maximum(tile(i) * hb - 1, 0), 0)),
        pl.BlockSpec((1, HALO, w), lambda b, i: (b, jnp.minimum((tile(i) + 1) * hb, nt * hb - 1), 0)),
        const((CONV_W, w)), const((1, w)), const((w, w)), const((1, w)), const((w, w)), const((1, w)), const((1, w)),
    ]
    args = [z0, z0, z0, conv_w, conv_b, w_r, b_r, w_i, b_i, lam]
    if reverse:
        in_specs += [tok, tok]
        args += [hf, gz1]
    return pl.pallas_call(
        functools.partial(_lru_kernel, nct=nct, nt=nt, reverse=reverse),
        grid=(bsz, nt),
        in_specs=in_specs,
        out_specs=tok,
        out_shape=jax.ShapeDtypeStruct((bsz, t, w), _BF if reverse else _F32),
        scratch_shapes=[pltpu.VMEM((TM + 2 * HALO, w), _F32), pltpu.VMEM((TM, w), _F32),
                        pltpu.VMEM((TM, w), _F32), pltpu.VMEM((8, w), _F32)],
        compiler_params=_params(("arbitrary", "arbitrary")),
        name="lru_bwd" if reverse else "lru_fwd",
    )(*args)


def _attend(qt, k_at, vt_at, v_rows, is_latent, n_ctx, n_lat, tk, stabilize):
    nq = qt.shape[1]

    def step(carry, start, size):
        st = _dot(k_at(start, size), qt)
        if stabilize:
            m, acc = carry
            m_new = jnp.maximum(m, jnp.max(st, axis=0, keepdims=True))
            acc = jnp.exp(m - m_new) * acc + _dot(vt_at(start, size), jnp.exp(st - m_new).astype(_BF))
            return m_new, acc
        return (carry[0] + _dot(vt_at(start, size), jnp.exp(st).astype(_BF)),)

    zero = jnp.zeros((v_rows, nq), _F32)
    init = (jnp.full((1, nq), -jnp.inf, _F32), zero) if stabilize else (zero,)

    def all_keys():
        carry = step(init, 0, n_ctx + tk)
        if stabilize:
            return lax.fori_loop(
                1, n_lat // tk, lambda j, c: step(c, pl.multiple_of(n_ctx + j * tk, math.gcd(n_ctx, tk)), tk), carry)
        for j in range(1, n_lat // tk):
            carry = step(carry, n_ctx + j * tk, tk)
        return carry

    return lax.cond(is_latent, all_keys, lambda: step(init, 0, n_ctx))[-1]


def _attend_guarded(bounded, qt, k_at, vt_at, v_rows, is_latent, n_ctx, n_lat, tk):
    return lax.cond(bounded,
                    lambda: _attend(qt, k_at, vt_at, v_rows, is_latent, n_ctx, n_lat, n_lat // 4, False),
                    lambda: _attend(qt, k_at, vt_at, v_rows, is_latent, n_ctx, n_lat, tk, True))


def _query_slots_t(q_ref):
    return jnp.concatenate([c.astype(_F32).T.astype(_BF) for c in _chunks(q_ref[0])], axis=1)


def _gqa_kernel(flag_ref, q_ref, k_ref, vt_ref, o_ref, *, nct, tk):
    tq = q_ref.shape[1]
    n_ctx = nct * TM
    acc = _attend_guarded(
        flag_ref[0] != 0, _query_slots_t(q_ref), lambda a, n: k_ref[0, pl.ds(a, n), :],
        lambda a, n: vt_ref[0, 0, :, pl.ds(a, n)], vt_ref.shape[2],
        pl.program_id(2) >= n_ctx // tq, n_ctx, k_ref.shape[1] - n_ctx, tk)
    ot = acc[:HEAD_DIM] / acc[HEAD_DIM:HEAD_DIM + 1]
    pairs = [jnp.concatenate([ot[:, 2 * p * tq:(2 * p + 1) * tq], ot[:, (2 * p + 1) * tq:(2 * p + 2) * tq]], axis=0).T
             for p in range(GQA_GROUP // 2)]
    o_ref[0] = jnp.concatenate(pairs, axis=-1).astype(o_ref.dtype)


def _gqa_attention(flag, qp, kn, vt, *, nct, tq, tk):
    bsz, t, _ = qp.shape
    gw = GQA_GROUP * LANES
    return pl.pallas_call(
        functools.partial(_gqa_kernel, nct=nct, tk=tk),
        grid_spec=pltpu.PrefetchScalarGridSpec(
            num_scalar_prefetch=1,
            grid=(bsz, GQA_KV_HEADS, t // tq),
            in_specs=[
                pl.BlockSpec((1, tq, gw), lambda b, g, i, f: (b, i, g)),
                pl.BlockSpec((1, t, LANES), lambda b, g, i, f: (b, 0, 0)),
                pl.BlockSpec((1, 1, vt.shape[2], t), lambda b, g, i, f: (b, g, 0, 0)),
            ],
            out_specs=pl.BlockSpec((1, tq, GQA_GROUP * HEAD_DIM), lambda b, g, i, f: (b, i, g)),
        ),
        out_shape=jax.ShapeDtypeStruct((bsz, t, GQA_HEADS * HEAD_DIM), _BF),
        compiler_params=_params(("arbitrary", "arbitrary", "arbitrary")),
        name="gqa_attention",
    )(flag, qp, kn, vt)


def _diff_kernel(flag_ref, q_ref, k_ref, vt_ref, lq1_ref, lk1_ref, lq2_ref, lk2_ref, g_ref, o_ref, *,
                 nct, tk, lam_init):
    tq = q_ref.shape[1]
    n_ctx = nct * TM
    acc = _attend_guarded(
        flag_ref[0] != 0, _query_slots_t(q_ref), lambda a, n: k_ref[0, pl.ds(a, n), :],
        lambda a, n: vt_ref[0, 0, :, pl.ds(a, n)], vt_ref.shape[2],
        pl.program_id(2) >= n_ctx // tq, n_ctx, k_ref.shape[1] - n_ctx, tk)
    lam = (jnp.exp(jnp.sum(lq1_ref[...] * lk1_ref[...], axis=-1, keepdims=True))
           - jnp.exp(jnp.sum(lq2_ref[...] * lk2_ref[...], axis=-1, keepdims=True)) + lam_init)
    a1 = acc[:, :tq]
    a2 = acc[:, tq:]
    ot = (a1[:DIFF_V_DIM] / a1[DIFF_V_DIM:DIFF_V_DIM + 1]
          - lam * (a2[:DIFF_V_DIM] / a2[DIFF_V_DIM:DIFF_V_DIM + 1]))
    ot = ot * lax.rsqrt(jnp.mean(ot * ot, axis=0, keepdims=True) + EPS) * g_ref[...]
    o_ref[0] = (ot * (1.0 - lam_init)).T.astype(o_ref.dtype)


def _diff_attention(flag, dqp, dkn, dvt, lq1, lk1, lq2, lk2, subln_g, *, nct, tq, tk, lam_init):
    bsz, t, _ = dqp.shape
    vec = pl.BlockSpec((1, HEAD_DIM), lambda b, n, i, f: (0, 0))
    return pl.pallas_call(
        functools.partial(_diff_kernel, nct=nct, tk=tk, lam_init=lam_init),
        grid_spec=pltpu.PrefetchScalarGridSpec(
            num_scalar_prefetch=1,
            grid=(bsz, DIFF_HEADS, t // tq),
            in_specs=[
                pl.BlockSpec((1, tq, 2 * LANES), lambda b, n, i, f: (b, i, n)),
                pl.BlockSpec((1, t, LANES), lambda b, n, i, f: (b, 0, n)),
                pl.BlockSpec((1, 1, dvt.shape[2], t), lambda b, n, i, f: (b, n, 0, 0)),
                vec, vec, vec, vec,
                pl.BlockSpec((DIFF_V_DIM, 1), lambda b, n, i, f: (0, 0)),
            ],
            out_specs=pl.BlockSpec((1, tq, DIFF_V_DIM), lambda b, n, i, f: (b, i, n)),
        ),
        out_shape=jax.ShapeDtypeStruct((bsz, t, DIFF_HEADS * DIFF_V_DIM), _BF),
        compiler_params=_params(("arbitrary", "arbitrary", "arbitrary")),
        name="diff_attention",
    )(flag, dqp, dkn, dvt, lq1, lk1, lq2, lk2, subln_g)


def _merge_kernel(*refs, n_x, nct, skip):
    x_refs = refs[:n_x]
    mod_ref, g1_ref, wg_ref, bg_ref, yr_ref, ya_ref, yd_ref, wb_ref, wo_ref, o_ref = refs[n_x:]
    x = _stream_tile(x_refs, pl.program_id(1) + skip, nct)
    d = x.shape[-1]
    hn = _rms_mod(x, g1_ref[...], mod_ref[0, :, d:2 * d], mod_ref[0, :, 0:d]).astype(_BF)
    gate = mod_ref[0, :, 2 * d:3 * d]
    ys = (yr_ref[0], ya_ref[0], yd_ref[0])
    m = None
    for n in range(N_BRANCH):
        g = _sigmoid(_dot(hn, wg_ref[:, n * d:(n + 1) * d]) + bg_ref[:, n * d:(n + 1) * d])
        term = g * _dot(ys[n], wb_ref[n])
        m = term if m is None else m + term
    o_ref[0] = x + gate * _dot(m.astype(_BF), wo_ref[...])


def _merge(xs, mod3, g1, w_gate, b_gate, y_rec, y_gqa, y_diff, w_branch, w_out, nct, skip):
    bsz, t, bw = y_rec.shape
    d = xs[0].shape[-1]
    nt = t // TM - skip
    row = _mod_row(nct, skip)
    tok = lambda w: pl.BlockSpec((1, TM, w), lambda b, i: (b, i + skip, 0))
    return pl.pallas_call(
        functools.partial(_merge_kernel, n_x=len(xs), nct=nct, skip=skip),
        grid=(bsz, nt),
        in_specs=_stream_specs(xs, skip, nct) + [
            pl.BlockSpec((1, 1, 3 * d), lambda b, i: (row(b, i, bsz), 0, 0)),
            pl.BlockSpec((1, d), lambda b, i: (0, 0)),
            _resident((d, N_BRANCH * d), lambda b, i: (0, 0)),
            pl.BlockSpec((1, N_BRANCH * d), lambda b, i: (0, 0)),
            tok(bw), tok(bw), tok(bw),
            _resident((N_BRANCH, bw, d), lambda b, i: (0, 0, 0)),
            _resident((d, d), lambda b, i: (0, 0)),
        ],
        out_specs=pl.BlockSpec((1, TM, d), lambda b, i: (b, i, 0)),
        out_shape=jax.ShapeDtypeStruct((bsz, nt * TM, d), _F32),
        compiler_params=_params(("arbitrary", "arbitrary")),
        name="merge",
    )(*xs, mod3, g1, w_gate, b_gate, y_rec, y_gqa, y_diff, w_branch, w_out)


def _mlp_kernel(x_ref, mod_ref, g2_ref, wu_ref, wd_ref, gf_ref, o_ref, *, final):
    d = x_ref.shape[-1]
    x = x_ref[0]
    h = _rms_mod(x, g2_ref[...], mod_ref[0, :, d:2 * d], mod_ref[0, :, 0:d]).astype(_BF)
    gate = mod_ref[0, :, 2 * d:3 * d]
    d_ff = wu_ref.shape[-1]
    acc = None
    for c in range(d_ff // d):
        u = jnp.maximum(_dot(h, wu_ref[:, c * d:(c + 1) * d]), 0.0)
        part = _dot((u * u).astype(_BF), wd_ref[c * d:(c + 1) * d, :])
        acc = part if acc is None else acc + part
    y = x + gate * acc
    if final:
        y = y * lax.rsqrt(jnp.mean(y * y, axis=-1, keepdims=True) + EPS) * gf_ref[...]
    o_ref[0] = y


def _mlp(xm, mod3, g2, w_up, w_down, final_g, nct, skip, final):
    bsz, t, d = xm.shape
    d_ff = w_up.shape[-1]
    row = _mod_row(nct, skip)
    tok = pl.BlockSpec((1, TM, d), lambda b, i: (b, i, 0))
    return pl.pallas_call(
        functools.partial(_mlp_kernel, final=final),
        grid=(bsz, t // TM),
        in_specs=[
            tok,
            pl.BlockSpec((1, 1, 3 * d), lambda b, i: (row(b, i, bsz), 0, 1)),
            pl.BlockSpec((1, d), lambda b, i: (0, 0)),
            _resident((d, d_ff), lambda b, i: (0, 0)),
            _resident((d_ff, d), lambda b, i: (0, 0)),
            pl.BlockSpec((1, d), lambda b, i: (0, 0)),
        ],
        out_specs=tok,
        out_shape=jax.ShapeDtypeStruct(xm.shape, _F32),
        compiler_params=_params(("arbitrary", "arbitrary")),
        name="mlp",
    )(xm, mod3, g2, w_up, w_down, final_g)


def _rope_tables(n_ctx, s):
    rows = s // GRID_W
    pos_r = jnp.repeat(jnp.arange(rows, dtype=_F32), GRID_W)
    pos_c = jnp.tile(jnp.arange(GRID_W, dtype=_F32), rows)
    n_freq = HEAD_DIM // 4
    inv = ROPE_THETA ** (-jnp.arange(n_freq, dtype=_F32) * 2.0 / (HEAD_DIM // 2))
    ang_r = pos_r[:, None] * inv
    ang_c = pos_c[:, None] * inv
    ang = jnp.concatenate([ang_r, ang_r, ang_c, ang_c], axis=-1)
    sign = jnp.tile(jnp.concatenate([-jnp.ones(n_freq, _F32), jnp.ones(n_freq, _F32)]), 2)
    cos = jnp.concatenate([jnp.ones((n_ctx, HEAD_DIM), _F32), jnp.cos(ang)], axis=0)
    sin = jnp.concatenate([jnp.zeros((n_ctx, HEAD_DIM), _F32), jnp.sin(ang) * sign], axis=0)
    return jnp.tile(cos, (1, LANES // HEAD_DIM)), jnp.tile(sin, (1, LANES // HEAD_DIM))


def _block_diag(w):
    nb, n, _ = w.shape
    eye = jnp.eye(nb, dtype=w.dtype)
    return jnp.einsum('ncd,nm->ncmd', w, eye).reshape(nb * n, nb * n)


def _forward(x, c, ctx, c_ctx, w_mod, b_mod, norm1_g, w_in, b_gate, conv_w, conv_b, w_rg, b_rg, w_ig, b_ig,
             lru_lambda, q_norm_g, k_norm_g, lambda_q1, lambda_k1, lambda_q2, lambda_k2, subln_g, w_branch,
             w_out, norm2_g, w_up, w_down, final_g, *, tq_gqa, tq_diff, tk):
    bsz, s, d = x.shape
    n_ctx = ctx.shape[1]
    depth = w_mod.shape[0]
    assert n_ctx % TM == 0 and n_ctx > 0 and s % TM == 0 and s % tk == 0 and bsz < MOD_ROWS
    assert TM % tq_gqa == 0 and TM % tq_diff == 0
    nct = n_ctx // TM
    t = n_ctx + s

    cin = jnp.zeros((MOD_ROWS, d), _F32).at[:bsz].set(c).at[bsz].set(c_ctx)
    mod = _modulation(cin, w_mod, b_mod)
    cos, sin = _rope_tables(n_ctx, s)
    bd = jnp.kron(jnp.eye(2 * LANES // HEAD_DIM, dtype=_F32), jnp.ones((HEAD_DIM, HEAD_DIM), _F32)).astype(_BF)
    xs = (ctx, x)

    for l in range(depth):
        last = l == depth - 1
        skip = nct if last else 0
        lam_init = 0.8 - 0.6 * math.exp(-0.3 * l)
        mod3 = mod[l].reshape(MOD_ROWS, 1, 6 * d)
        w_proj = w_in[l, :, :N_PROJ].astype(_BF)
        w_gate = w_in[l, :, N_PROJ:].astype(_BF)
        gq = jnp.tile(q_norm_g[l], LANES // HEAD_DIM).reshape(1, LANES)
        gk = jnp.tile(k_norm_g[l], LANES // HEAD_DIM).reshape(1, LANES)
        z0, gz1, qp, kn, vt, dqp, dkn, dvt, stats = _inproj(
            xs, mod3, norm1_g[l].reshape(1, d), w_proj, gq, gk, cos, sin, bd, nct, t)

        lru = lambda dr, rev, **kw: _lru_sweep(
            z0, conv_w[l], conv_b[l].reshape(1, -1), _block_diag(w_rg[l, dr]).astype(_BF), b_rg[l, dr].reshape(1, -1),
            _block_diag(w_ig[l, dr]).astype(_BF), b_ig[l, dr].reshape(1, -1), lru_lambda[l, dr].reshape(1, -1),
            nct, rev, **kw)
        hf = lru(0, False)
        y_rec = lru(1, True, hf=hf, gz1=gz1)

        st = jnp.max(stats, axis=(0, 1, 3))
        gqa_bound = math.sqrt(HEAD_DIM) * jnp.max(jnp.abs(q_norm_g[l])) * jnp.max(jnp.abs(k_norm_g[l]))
        gqa_ok = jnp.logical_and(gqa_bound * BOUND_SLACK <= SAFE_LOGIT, st[2] <= SAFE_VALUE)
        diff_ok = jnp.logical_and(st[0] * st[1] * BOUND_SLACK <= SAFE_LOGIT * SAFE_LOGIT, st[3] <= SAFE_VALUE)
        y_gqa = _gqa_attention(gqa_ok.astype(jnp.int32).reshape(1), qp, kn, vt, nct=nct, tq=tq_gqa, tk=tk)
        vec = lambda a: a[l].reshape(1, HEAD_DIM)
        y_diff = _diff_attention(diff_ok.astype(jnp.int32).reshape(1), dqp, dkn, dvt, vec(lambda_q1), vec(lambda_k1),
                                 vec(lambda_q2), vec(lambda_k2), subln_g[l].reshape(DIFF_V_DIM, 1),
                                 nct=nct, tq=tq_diff, tk=tk, lam_init=lam_init)

        xm = _merge(xs, mod3, norm1_g[l].reshape(1, d), w_gate, b_gate[l].reshape(1, -1), y_rec, y_gqa, y_diff,
                    w_branch[l].astype(_BF), w_out[l].astype(_BF), nct, skip)
        xs = (_mlp(xm, mod3, norm2_g[l].reshape(1, d), w_up[l].astype(_BF), w_down[l].astype(_BF),
                   final_g.reshape(1, d), nct, skip, last),)
    return xs[0]


def kernel(x, c, ctx, c_ctx, w_mod, b_mod, norm1_g, w_in, b_gate, conv_w, conv_b, w_rg, b_rg, w_ig, b_ig, lru_lambda, q_norm_g, k_norm_g, lambda_q1, lambda_k1, lambda_q2, lambda_k2, subln_g, w_branch, w_out, norm2_g, w_up, w_down, final_g):
    return _forward(x, c, ctx, c_ctx, w_mod, b_mod, norm1_g, w_in, b_gate, conv_w, conv_b, w_rg, b_rg, w_ig, b_ig,
                    lru_lambda, q_norm_g, k_norm_g, lambda_q1, lambda_k1, lambda_q2, lambda_k2, subln_g, w_branch,
                    w_out, norm2_g, w_up, w_down, final_g, tq_gqa=256, tq_diff=256, tk=2048)
```

```python
import functools
import math

import jax
import jax.numpy as jnp
from jax import lax
from jax.experimental import pallas as pl
from jax.experimental.pallas import tpu as pltpu

GRID_W = 64
HEAD_DIM = 64
LRU_WIDTH = 512
LRU_BLOCKS = 8
CONV_W = 4
CONV_LEFT = 2
LRU_C = 8.0
GQA_HEADS = 8
GQA_KV_HEADS = 2
GQA_GROUP = GQA_HEADS // GQA_KV_HEADS
DIFF_HEADS = 4
DIFF_V_DIM = 2 * HEAD_DIM
N_BRANCH = 3
ROPE_THETA = 10000.0
EPS = 1e-6
SCALE = HEAD_DIM ** -0.5

TM = 256
HALO = 8
LANES = 128
MOD_ROWS = 8
ONES_ROWS = 16
VMEM_LIMIT = 56 * 1024 * 1024

SEC_LRU = 2 * LRU_WIDTH
SEC_Q = GQA_HEADS * HEAD_DIM
SEC_KV = 2 * GQA_KV_HEADS * HEAD_DIM
SEC_DQ = DIFF_HEADS * 2 * HEAD_DIM
SEC_DV = DIFF_HEADS * DIFF_V_DIM
N_PROJ = SEC_LRU + SEC_Q + SEC_KV + 2 * SEC_DQ + SEC_DV

SAFE_LOGIT = 60.0
SAFE_VALUE = 1.0e6
BOUND_SLACK = 1.05

_BF = jnp.bfloat16
_F32 = jnp.float32


def _params(sem):
    return pltpu.CompilerParams(dimension_semantics=sem, vmem_limit_bytes=VMEM_LIMIT)


def _resident(shape, index_map):
    return pl.BlockSpec(shape, index_map, pipeline_mode=pl.Buffered(1))


def _split_bf16(a):
    hi = a.astype(_BF)
    lo = (a - hi.astype(_F32)).astype(_BF)
    return hi, lo


def _dot(a, b):
    return jnp.dot(a, b, preferred_element_type=_F32)


def _sigmoid(x):
    return 0.5 * jnp.tanh(0.5 * x) + 0.5


def _mod_kernel(c_ref, w_ref, b_ref, o_ref):
    c = c_ref[...]
    s = c * _sigmoid(c)
    s_hi, s_lo = _split_bf16(s)
    w_hi, w_lo = _split_bf16(w_ref[...])
    o_ref[...] = _dot(s_hi, w_hi) + (_dot(s_lo, w_hi) + _dot(s_hi, w_lo)) + b_ref[...]


def _modulation(cin, w_mod, b_mod):
    depth, d, n = w_mod.shape
    tn = 1536
    return pl.pallas_call(
        _mod_kernel,
        grid=(depth, n // tn),
        in_specs=[
            pl.BlockSpec((MOD_ROWS, d), lambda l, j: (0, 0)),
            pl.BlockSpec((None, d, tn), lambda l, j: (l, 0, j)),
            pl.BlockSpec((None, 1, tn), lambda l, j: (l, 0, j)),
        ],
        out_specs=pl.BlockSpec((None, MOD_ROWS, tn), lambda l, j: (l, 0, j)),
        out_shape=jax.ShapeDtypeStruct((depth, MOD_ROWS, n), _F32),
        compiler_params=_params(("arbitrary", "arbitrary")),
        name="modulation",
    )(cin, w_mod, b_mod.reshape(depth, 1, n))


def _rms_mod(x, g, scale, shift):
    y = x * lax.rsqrt(jnp.mean(x * x, axis=-1, keepdims=True) + EPS) * g
    return y * (1.0 + scale) + shift


def _swap16(y):
    lane = lax.broadcasted_iota(jnp.int32, y.shape, 1)
    return jnp.where(lane % 32 < 16, pltpu.roll(y, LANES - 16, 1), pltpu.roll(y, 16, 1))


def _rope(y, cos, sin):
    return y * cos + _swap16(y) * sin


def _chunks(z):
    return [z[:, j * LANES:(j + 1) * LANES] for j in range(z.shape[-1] // LANES)]


def _mod_row(nct, skip):
    def row(b, i, nb):
        return jnp.where(i + skip < nct, nb, b)
    return row


def _stream_specs(xs, skip, nct):
    d = xs[0].shape[-1]
    if len(xs) == 1:
        return [pl.BlockSpec((1, TM, d), lambda b, i: (b, i + skip, 0))]
    return [pl.BlockSpec((1, TM, d), lambda b, i: (b, jnp.minimum(i + skip, nct - 1), 0)),
            pl.BlockSpec((1, TM, d), lambda b, i: (b, jnp.maximum(i + skip - nct, 0), 0))]


def _stream_tile(refs, tile, nct):
    if len(refs) == 1:
        return refs[0][0]
    return jnp.where(tile < nct, refs[0][0], refs[1][0])


def _inproj_kernel(*refs, n_x, nct):
    x_refs = refs[:n_x]
    (mod_ref, g1_ref, w_ref, gq_ref, gk_ref, cos_ref, sin_ref, bd_ref,
     z0_ref, gz1_ref, qp_ref, kn_ref, vt_ref, dqp_ref, dkn_ref, dvt_ref, st_ref) = refs[n_x:]
    x = _stream_tile(x_refs, pl.program_id(1), nct)
    d = x.shape[-1]
    hn = _rms_mod(x, g1_ref[...], mod_ref[0, :, d:2 * d], mod_ref[0, :, 0:d]).astype(_BF)
    cos = cos_ref[...]
    sin = sin_ref[...]
    bd = bd_ref[...]
    lane = lax.broadcasted_iota(jnp.int32, (TM, LANES), 1)
    lo_half = lane < HEAD_DIM

    def proj(a, width):
        return _dot(hn, w_ref[:, a:a + width])

    def head_sumsq(z, exact):
        parts = []
        for h in range(z.shape[-1] // (2 * LANES)):
            sq = z[:, h * 2 * LANES:(h + 1) * 2 * LANES]
            sq = sq * sq
            if exact:
                hi, lo = _split_bf16(sq)
                parts.append(_dot(hi, bd) + _dot(lo, bd))
            else:
                parts.append(_dot(sq.astype(_BF), bd))
        return parts[0] if len(parts) == 1 else jnp.concatenate(parts, axis=-1)

    def amax(a):
        return jnp.max(jnp.max(a, axis=0, keepdims=True), axis=1, keepdims=True)

    c0 = SEC_LRU

    zq = proj(c0, SEC_Q)
    c0 += SEC_Q
    inv = lax.rsqrt(head_sumsq(zq, True) * (1.0 / HEAD_DIM) + EPS)
    for j, (c, r) in enumerate(zip(_chunks(zq), _chunks(inv))):
        val = _rope(c * r * gq_ref[...], cos, sin) * SCALE
        swapped = pltpu.roll(val, HEAD_DIM, 1)
        if (2 * j) // GQA_GROUP == 0:
            slots = (jnp.where(lo_half, val, 0.0), jnp.where(lo_half, swapped, 0.0))
        else:
            slots = (jnp.where(lo_half, 0.0, swapped), jnp.where(lo_half, 0.0, val))
        for n, slot in enumerate(slots):
            qp_ref[0, :, (2 * j + n) * LANES:(2 * j + n + 1) * LANES] = slot.astype(_BF)

    zkv = proj(c0, SEC_KV)
    c0 += SEC_KV
    inv = lax.rsqrt(head_sumsq(zkv, True)[:, :LANES] * (1.0 / HEAD_DIM) + EPS)
    kn_ref[0] = _rope(zkv[:, :LANES] * inv * gk_ref[...], cos, sin).astype(_BF)
    v = zkv[:, LANES:]
    ones_rows = (lax.broadcasted_iota(jnp.int32, (ONES_ROWS, TM), 0) == 0).astype(_BF)
    vt = v.T.astype(_BF)
    for g in range(GQA_KV_HEADS):
        vt_ref[0, g, 0:HEAD_DIM] = vt[g * HEAD_DIM:(g + 1) * HEAD_DIM]
        vt_ref[0, g, HEAD_DIM:HEAD_DIM + ONES_ROWS] = ones_rows

    zdq = proj(c0, SEC_DQ)
    c0 += SEC_DQ
    for n, c in enumerate(_chunks(zdq)):
        val = _rope(c, cos, sin) * SCALE
        dqp_ref[0, :, 2 * n * LANES:(2 * n + 1) * LANES] = jnp.where(lo_half, val, 0.0).astype(_BF)
        dqp_ref[0, :, (2 * n + 1) * LANES:(2 * n + 2) * LANES] = jnp.where(lo_half, 0.0, val).astype(_BF)
    zdk = proj(c0, SEC_DQ)
    c0 += SEC_DQ
    for n, c in enumerate(_chunks(zdk)):
        dkn_ref[0, :, n * LANES:(n + 1) * LANES] = _rope(c, cos, sin).astype(_BF)
    zdv = proj(c0, SEC_DV)
    for n, c in enumerate(_chunks(zdv)):
        dvt_ref[0, n, 0:DIFF_V_DIM] = c.T.astype(_BF)
        dvt_ref[0, n, DIFF_V_DIM:DIFF_V_DIM + ONES_ROWS] = ones_rows

    stats = (amax(head_sumsq(zdq, False)) * (SCALE * SCALE), amax(head_sumsq(zdk, False)),
             amax(jnp.abs(v)), amax(jnp.abs(zdv)))
    row = lax.broadcasted_iota(jnp.int32, (8, LANES), 0)
    st = jnp.zeros((8, LANES), _F32)
    for n, val in enumerate(stats):
        st = jnp.where(row == n, val, st)
    st_ref[0, 0] = st

    gz1_ref[0] = jax.nn.gelu(proj(LRU_WIDTH, LRU_WIDTH))
    z0_ref[0] = proj(0, LRU_WIDTH)


def _inproj(xs, mod3, g1, w_proj, gq, gk, cos, sin, bd, nct, t):
    bsz = xs[0].shape[0]
    d = xs[0].shape[-1]
    nt = t // TM
    row = _mod_row(nct, 0)
    tok = lambda w: pl.BlockSpec((1, TM, w), lambda b, i: (b, i, 0))
    const = lambda shape: pl.BlockSpec(shape, lambda b, i: (0,) * len(shape))
    out_specs = [
        tok(LRU_WIDTH), tok(LRU_WIDTH), tok(GQA_HEADS * LANES),
        tok(LANES),
        pl.BlockSpec((1, GQA_KV_HEADS, HEAD_DIM + ONES_ROWS, TM), lambda b, i: (b, 0, 0, i)),
        tok(2 * DIFF_HEADS * LANES),
        tok(DIFF_HEADS * LANES),
        pl.BlockSpec((1, DIFF_HEADS, DIFF_V_DIM + ONES_ROWS, TM), lambda b, i: (b, 0, 0, i)),
        pl.BlockSpec((1, 1, 8, LANES), lambda b, i: (b, i, 0, 0)),
    ]
    out_shape = [
        jax.ShapeDtypeStruct((bsz, t, LRU_WIDTH), _F32), jax.ShapeDtypeStruct((bsz, t, LRU_WIDTH), _F32),
        jax.ShapeDtypeStruct((bsz, t, GQA_HEADS * LANES), _BF),
        jax.ShapeDtypeStruct((bsz, t, LANES), _BF),
        jax.ShapeDtypeStruct((bsz, GQA_KV_HEADS, HEAD_DIM + ONES_ROWS, t), _BF),
        jax.ShapeDtypeStruct((bsz, t, 2 * DIFF_HEADS * LANES), _BF),
        jax.ShapeDtypeStruct((bsz, t, DIFF_HEADS * LANES), _BF),
        jax.ShapeDtypeStruct((bsz, DIFF_HEADS, DIFF_V_DIM + ONES_ROWS, t), _BF),
        jax.ShapeDtypeStruct((bsz, nt, 8, LANES), _F32),
    ]
    return pl.pallas_call(
        functools.partial(_inproj_kernel, n_x=len(xs), nct=nct),
        grid=(bsz, nt),
        in_specs=_stream_specs(xs, 0, nct) + [
            pl.BlockSpec((1, 1, 2 * d), lambda b, i: (row(b, i, bsz), 0, 0)),
            const((1, d)),
            _resident((d, N_PROJ), lambda b, i: (0, 0)),
            const((1, LANES)), const((1, LANES)),
            pl.BlockSpec((TM, LANES), lambda b, i: (i, 0)),
            pl.BlockSpec((TM, LANES), lambda b, i: (i, 0)),
            const((2 * LANES, 2 * LANES)),
        ],
        out_specs=out_specs,
        out_shape=out_shape,
        compiler_params=_params(("arbitrary", "arbitrary")),
        name="inproj",
    )(*xs, mod3, g1, w_proj, gq, gk, cos, sin, bd)


def _lru_kernel(*refs, nct, nt, reverse):
    if reverse:
        y_ref, wr_ref, br_ref, wi_ref, bi_ref, lam_ref, hf_ref, gz1_ref, o_ref, a_ref, b_ref, h_ref = refs
    else:
        (z_ref, zp_ref, zn_ref, cw_ref, cb_ref, wr_ref, br_ref, wi_ref, bi_ref, lam_ref,
         o_ref, y_ref, ext_ref, a_ref, b_ref, h_ref) = refs
    i = pl.program_id(1)
    tile = _lru_tile(i, nct, nt, reverse)

    @pl.when(i == 0)
    def _():
        h_ref[...] = jnp.zeros_like(h_ref)

    if reverse:
        y = y_ref[0]
    else:
        seg_first = jnp.logical_or(tile == 0, tile == nct)
        seg_last = jnp.logical_or(tile == nct - 1, tile == nt - 1)
        u = z_ref[0]
        ext_ref[0:HALO, :] = jnp.where(seg_first, 0.0, zp_ref[0])
        ext_ref[HALO:HALO + TM, :] = u
        ext_ref[HALO + TM:, :] = jnp.where(seg_last, 0.0, zn_ref[0])
        y = cb_ref[...] + cw_ref[CONV_LEFT:CONV_LEFT + 1, :] * u
        for j in range(CONV_W):
            if j != CONV_LEFT:
                y = y + cw_ref[j:j + 1, :] * ext_ref[pl.ds(HALO + j - CONV_LEFT, TM), :]
        y_ref[0] = y

    yb = y.astype(_BF)
    r = _sigmoid(_dot(yb, wr_ref[...]) + br_ref[...])
    ig = _sigmoid(_dot(yb, wi_ref[...]) + bi_ref[...])
    log_a = r * ((-LRU_C) * jax.nn.softplus(-lam_ref[...]))
    a = jnp.exp(log_a)
    a_ref[...] = a
    b_ref[...] = jnp.sqrt(-jnp.tanh(log_a) * (a * a + 1.0)) * (ig * y)

    ng = TM // 8
    row = lax.broadcasted_iota(jnp.int32, (8, LRU_WIDTH), 0)

    def body(g, h):
        r0 = pl.multiple_of((ng - 1 - g if reverse else g) * 8, 8)
        aa = a_ref[pl.ds(r0, 8), :]
        bb = b_ref[pl.ds(r0, 8), :]
        for s in (1, 2, 4):
            sh = 8 - s if reverse else s
            m = (row < 8 - s) if reverse else (row >= s)
            bb = jnp.where(m, aa * pltpu.roll(bb, sh, 0) + bb, bb)
            aa = jnp.where(m, aa * pltpu.roll(aa, sh, 0), aa)
        hh = aa * h + bb
        if reverse:
            hsum = hf_ref[0, pl.ds(r0, 8), :] + hh
            o_ref[0, pl.ds(r0, 8), :] = (hsum * gz1_ref[0, pl.ds(r0, 8), :]).astype(o_ref.dtype)
            return hh[0:1, :]
        o_ref[0, pl.ds(r0, 8), :] = hh
        return hh[7:8, :]

    h_ref[0:1, :] = lax.fori_loop(0, ng, body, h_ref[0:1, :], unroll=4)


def _lru_tile(i, nct, nt, reverse):
    if not reverse:
        return i
    return jnp.where(i < nct, nct - 1 - i, nt - 1 - (i - nct))


def _lru_sweep(src, gates, nct, reverse, conv=None, hf=None, gz1=None):
    bsz, t, w = src.shape
    nt = t // TM
    hb = TM // HALO
    tile = lambda i: _lru_tile(i, nct, nt, reverse)
    tok = pl.BlockSpec((1, TM, w), lambda b, i: (b, tile(i), 0))
    const = lambda shape: pl.BlockSpec(shape, lambda b, i: (0,) * len(shape))
    gate_specs = [const((w, w)), const((1, w)), const((w, w)), const((1, w)), const((1, w))]
    scratch = [pltpu.VMEM((TM, w), _F32), pltpu.VMEM((TM, w), _F32), pltpu.VMEM((8, w), _F32)]
    if reverse:
        in_specs = [tok] + gate_specs + [tok, tok]
        args = [src, *gates, hf, gz1]
        out_specs, out_shape = tok, jax.ShapeDtypeStruct((bsz, t, w), _BF)
    else:
        in_specs = [
            tok,
            pl.BlockSpec((1, HALO, w), lambda b, i: (b, jnp.maximum(tile(i) * hb - 1, 0), 0)),
            pl.BlockSpec((1, HALO, w), lambda b, i: (b, jnp.minimum((tile(i) + 1) * hb, nt * hb - 1), 0)),
            const((CONV_W, w)), const((1, w)),
        ] + gate_specs
        args = [src, src, src, *conv, *gates]
        out_specs = [tok, tok]
        out_shape = [jax.ShapeDtypeStruct((bsz, t, w), _F32)] * 2
        scratch = [pltpu.VMEM((TM + 2 * HALO, w), _F32)] + scratch
    return pl.pallas_call(
        functools.partial(_lru_kernel, nct=nct, nt=nt, reverse=reverse),
        grid=(bsz, nt),
        in_specs=in_specs,
        out_specs=out_specs,
        out_shape=out_shape,
        scratch_shapes=scratch,
        compiler_params=_params(("arbitrary", "arbitrary")),
        name="lru_bwd" if reverse else "lru_fwd",
    )(*args)


def _attend(qt, k_at, vt_at, v_rows, is_latent, n_ctx, n_lat, tk, stabilize):
    nq = qt.shape[1]

    def step(carry, start, size):
        st = _dot(k_at(start, size), qt)
        if stabilize:
            m, acc = carry
            m_new = jnp.maximum(m, jnp.max(st, axis=0, keepdims=True))
            acc = jnp.exp(m - m_new) * acc + _dot(vt_at(start, size), jnp.exp(st - m_new).astype(_BF))
            return m_new, acc
        return (carry[0] + _dot(vt_at(start, size), jnp.exp(st).astype(_BF)),)

    zero = jnp.zeros((v_rows, nq), _F32)
    init = (jnp.full((1, nq), -jnp.inf, _F32), zero) if stabilize else (zero,)

    def all_keys():
        carry = step(init, 0, n_ctx + tk)
        if stabilize:
            return lax.fori_loop(
                1, n_lat // tk, lambda j, c: step(c, pl.multiple_of(n_ctx + j * tk, math.gcd(n_ctx, tk)), tk), carry)
        for j in range(1, n_lat // tk):
            carry = step(carry, n_ctx + j * tk, tk)
        return carry

    return lax.cond(is_latent, all_keys, lambda: step(init, 0, n_ctx))[-1]


def _attend_guarded(bounded, qt, k_at, vt_at, v_rows, is_latent, n_ctx, n_lat, tk):
    return lax.cond(bounded,
                    lambda: _attend(qt, k_at, vt_at, v_rows, is_latent, n_ctx, n_lat, n_lat // 4, False),
                    lambda: _attend(qt, k_at, vt_at, v_rows, is_latent, n_ctx, n_lat, tk, True))


def _query_slots_t(q_ref):
    return jnp.concatenate([c.astype(_F32).T.astype(_BF) for c in _chunks(q_ref[0])], axis=1)


def _gqa_kernel(flag_ref, q_ref, k_ref, vt_ref, o_ref, *, nct, tk):
    tq = q_ref.shape[1]
    n_ctx = nct * TM
    acc = _attend_guarded(
        flag_ref[0] != 0, _query_slots_t(q_ref), lambda a, n: k_ref[0, pl.ds(a, n), :],
        lambda a, n: vt_ref[0, 0, :, pl.ds(a, n)], vt_ref.shape[2],
        pl.program_id(2) >= n_ctx // tq, n_ctx, k_ref.shape[1] - n_ctx, tk)
    ot = acc[:HEAD_DIM] / acc[HEAD_DIM:HEAD_DIM + 1]
    pairs = [jnp.concatenate([ot[:, 2 * p * tq:(2 * p + 1) * tq], ot[:, (2 * p + 1) * tq:(2 * p + 2) * tq]], axis=0).T
             for p in range(GQA_GROUP // 2)]
    o_ref[0] = jnp.concatenate(pairs, axis=-1).astype(o_ref.dtype)


def _gqa_attention(flag, qp, kn, vt, *, nct, tq, tk):
    bsz, t, _ = qp.shape
    gw = GQA_GROUP * LANES
    return pl.pallas_call(
        functools.partial(_gqa_kernel, nct=nct, tk=tk),
        grid_spec=pltpu.PrefetchScalarGridSpec(
            num_scalar_prefetch=1,
            grid=(bsz, GQA_KV_HEADS, t // tq),
            in_specs=[
                pl.BlockSpec((1, tq, gw), lambda b, g, i, f: (b, i, g)),
                pl.BlockSpec((1, t, LANES), lambda b, g, i, f: (b, 0, 0)),
                pl.BlockSpec((1, 1, vt.shape[2], t), lambda b, g, i, f: (b, g, 0, 0)),
            ],
            out_specs=pl.BlockSpec((1, tq, GQA_GROUP * HEAD_DIM), lambda b, g, i, f: (b, i, g)),
        ),
        out_shape=jax.ShapeDtypeStruct((bsz, t, GQA_HEADS * HEAD_DIM), _BF),
        compiler_params=_params(("arbitrary", "arbitrary", "arbitrary")),
        name="gqa_attention",
    )(flag, qp, kn, vt)


def _diff_kernel(flag_ref, q_ref, k_ref, vt_ref, lq1_ref, lk1_ref, lq2_ref, lk2_ref, g_ref, o_ref, *,
                 nct, tk, lam_init):
    tq = q_ref.shape[1]
    n_ctx = nct * TM
    acc = _attend_guarded(
        flag_ref[0] != 0, _query_slots_t(q_ref), lambda a, n: k_ref[0, pl.ds(a, n), :],
        lambda a, n: vt_ref[0, 0, :, pl.ds(a, n)], vt_ref.shape[2],
        pl.program_id(2) >= n_ctx // tq, n_ctx, k_ref.shape[1] - n_ctx, tk)
    lam = (jnp.exp(jnp.sum(lq1_ref[...] * lk1_ref[...], axis=-1, keepdims=True))
           - jnp.exp(jnp.sum(lq2_ref[...] * lk2_ref[...], axis=-1, keepdims=True)) + lam_init)
    a1 = acc[:, :tq]
    a2 = acc[:, tq:]
    ot = (a1[:DIFF_V_DIM] / a1[DIFF_V_DIM:DIFF_V_DIM + 1]
          - lam * (a2[:DIFF_V_DIM] / a2[DIFF_V_DIM:DIFF_V_DIM + 1]))
    ot = ot * lax.rsqrt(jnp.mean(ot * ot, axis=0, keepdims=True) + EPS) * g_ref[...]
    o_ref[0] = (ot * (1.0 - lam_init)).T.astype(o_ref.dtype)


def _diff_attention(flag, dqp, dkn, dvt, lq1, lk1, lq2, lk2, subln_g, *, nct, tq, tk, lam_init):
    bsz, t, _ = dqp.shape
    vec = pl.BlockSpec((1, HEAD_DIM), lambda b, n, i, f: (0, 0))
    return pl.pallas_call(
        functools.partial(_diff_kernel, nct=nct, tk=tk, lam_init=lam_init),
        grid_spec=pltpu.PrefetchScalarGridSpec(
            num_scalar_prefetch=1,
            grid=(bsz, DIFF_HEADS, t // tq),
            in_specs=[
                pl.BlockSpec((1, tq, 2 * LANES), lambda b, n, i, f: (b, i, n)),
                pl.BlockSpec((1, t, LANES), lambda b, n, i, f: (b, 0, n)),
                pl.BlockSpec((1, 1, dvt.shape[2], t), lambda b, n, i, f: (b, n, 0, 0)),
                vec, vec, vec, vec,
                pl.BlockSpec((DIFF_V_DIM, 1), lambda b, n, i, f: (0, 0)),
            ],
            out_specs=pl.BlockSpec((1, tq, DIFF_V_DIM), lambda b, n, i, f: (b, i, n)),
        ),
        out_shape=jax.ShapeDtypeStruct((bsz, t, DIFF_HEADS * DIFF_V_DIM), _BF),
        compiler_params=_params(("arbitrary", "arbitrary", "arbitrary")),
        name="diff_attention",
    )(flag, dqp, dkn, dvt, lq1, lk1, lq2, lk2, subln_g)


def _merge_kernel(*refs, n_x, nct, skip):
    x_refs = refs[:n_x]
    mod_ref, g1_ref, wg_ref, bg_ref, yr_ref, ya_ref, yd_ref, wb_ref, wo_ref, o_ref = refs[n_x:]
    x = _stream_tile(x_refs, pl.program_id(1) + skip, nct)
    d = x.shape[-1]
    hn = _rms_mod(x, g1_ref[...], mod_ref[0, :, d:2 * d], mod_ref[0, :, 0:d]).astype(_BF)
    gate = mod_ref[0, :, 2 * d:3 * d]
    ys = (yr_ref[0], ya_ref[0], yd_ref[0])
    m = None
    for n in range(N_BRANCH):
        g = _sigmoid(_dot(hn, wg_ref[:, n * d:(n + 1) * d]) + bg_ref[:, n * d:(n + 1) * d])
        term = g * _dot(ys[n], wb_ref[n])
        m = term if m is None else m + term
    o_ref[0] = x + gate * _dot(m.astype(_BF), wo_ref[...])


def _merge(xs, mod3, g1, w_gate, b_gate, y_rec, y_gqa, y_diff, w_branch, w_out, nct, skip):
    bsz, t, bw = y_rec.shape
    d = xs[0].shape[-1]
    nt = t // TM - skip
    row = _mod_row(nct, skip)
    tok = lambda w: pl.BlockSpec((1, TM, w), lambda b, i: (b, i + skip, 0))
    return pl.pallas_call(
        functools.partial(_merge_kernel, n_x=len(xs), nct=nct, skip=skip),
        grid=(bsz, nt),
        in_specs=_stream_specs(xs, skip, nct) + [
            pl.BlockSpec((1, 1, 3 * d), lambda b, i: (row(b, i, bsz), 0, 0)),
            pl.BlockSpec((1, d), lambda b, i: (0, 0)),
            _resident((d, N_BRANCH * d), lambda b, i: (0, 0)),
            pl.BlockSpec((1, N_BRANCH * d), lambda b, i: (0, 0)),
            tok(bw), tok(bw), tok(bw),
            _resident((N_BRANCH, bw, d), lambda b, i: (0, 0, 0)),
            _resident((d, d), lambda b, i: (0, 0)),
        ],
        out_specs=pl.BlockSpec((1, TM, d), lambda b, i: (b, i, 0)),
        out_shape=jax.ShapeDtypeStruct((bsz, nt * TM, d), _F32),
        compiler_params=_params(("arbitrary", "arbitrary")),
        name="merge",
    )(*xs, mod3, g1, w_gate, b_gate, y_rec, y_gqa, y_diff, w_branch, w_out)


def _mlp_kernel(x_ref, mod_ref, g2_ref, wu_ref, wd_ref, gf_ref, o_ref, *, final):
    d = x_ref.shape[-1]
    x = x_ref[0]
    h = _rms_mod(x, g2_ref[...], mod_ref[0, :, d:2 * d], mod_ref[0, :, 0:d]).astype(_BF)
    gate = mod_ref[0, :, 2 * d:3 * d]
    d_ff = wu_ref.shape[-1]
    acc = None
    for c in range(d_ff // d):
        u = jnp.maximum(_dot(h, wu_ref[:, c * d:(c + 1) * d]), 0.0)
        part = _dot((u * u).astype(_BF), wd_ref[c * d:(c + 1) * d, :])
        acc = part if acc is None else acc + part
    y = x + gate * acc
    if final:
        y = y * lax.rsqrt(jnp.mean(y * y, axis=-1, keepdims=True) + EPS) * gf_ref[...]
    o_ref[0] = y


def _mlp(xm, mod3, g2, w_up, w_down, final_g, nct, skip, final):
    bsz, t, d = xm.shape
    d_ff = w_up.shape[-1]
    row = _mod_row(nct, skip)
    tok = pl.BlockSpec((1, TM, d), lambda b, i: (b, i, 0))
    return pl.pallas_call(
        functools.partial(_mlp_kernel, final=final),
        grid=(bsz, t // TM),
        in_specs=[
            tok,
            pl.BlockSpec((1, 1, 3 * d), lambda b, i: (row(b, i, bsz), 0, 1)),
            pl.BlockSpec((1, d), lambda b, i: (0, 0)),
            _resident((d, d_ff), lambda b, i: (0, 0)),
            _resident((d_ff, d), lambda b, i: (0, 0)),
            pl.BlockSpec((1, d), lambda b, i: (0, 0)),
        ],
        out_specs=tok,
        out_shape=jax.ShapeDtypeStruct(xm.shape, _F32),
        compiler_params=_params(("arbitrary", "arbitrary")),
        name="mlp",
    )(xm, mod3, g2, w_up, w_down, final_g)


def _rope_tables(n_ctx, s):
    rows = s // GRID_W
    pos_r = jnp.repeat(jnp.arange(rows, dtype=_F32), GRID_W)
    pos_c = jnp.tile(jnp.arange(GRID_W, dtype=_F32), rows)
    n_freq = HEAD_DIM // 4
    inv = ROPE_THETA ** (-jnp.arange(n_freq, dtype=_F32) * 2.0 / (HEAD_DIM // 2))
    ang_r = pos_r[:, None] * inv
    ang_c = pos_c[:, None] * inv
    ang = jnp.concatenate([ang_r, ang_r, ang_c, ang_c], axis=-1)
    sign = jnp.tile(jnp.concatenate([-jnp.ones(n_freq, _F32), jnp.ones(n_freq, _F32)]), 2)
    cos = jnp.concatenate([jnp.ones((n_ctx, HEAD_DIM), _F32), jnp.cos(ang)], axis=0)
    sin = jnp.concatenate([jnp.zeros((n_ctx, HEAD_DIM), _F32), jnp.sin(ang) * sign], axis=0)
    return jnp.tile(cos, (1, LANES // HEAD_DIM)), jnp.tile(sin, (1, LANES // HEAD_DIM))


def _block_diag(w):
    nb, n, _ = w.shape
    eye = jnp.eye(nb, dtype=w.dtype)
    return jnp.einsum('ncd,nm->ncmd', w, eye).reshape(nb * n, nb * n)


def _forward(x, c, ctx, c_ctx, w_mod, b_mod, norm1_g, w_in, b_gate, conv_w, conv_b, w_rg, b_rg, w_ig, b_ig,
             lru_lambda, q_norm_g, k_norm_g, lambda_q1, lambda_k1, lambda_q2, lambda_k2, subln_g, w_branch,
             w_out, norm2_g, w_up, w_down, final_g, *, tq_gqa, tq_diff, tk):
    bsz, s, d = x.shape
    n_ctx = ctx.shape[1]
    depth = w_mod.shape[0]
    assert n_ctx % TM == 0 and n_ctx > 0 and s % TM == 0 and s % tk == 0 and bsz < MOD_ROWS
    assert TM % tq_gqa == 0 and TM % tq_diff == 0
    nct = n_ctx // TM
    t = n_ctx + s

    cin = jnp.zeros((MOD_ROWS, d), _F32).at[:bsz].set(c).at[bsz].set(c_ctx)
    mod = _modulation(cin, w_mod, b_mod)
    cos, sin = _rope_tables(n_ctx, s)
    bd = jnp.kron(jnp.eye(2 * LANES // HEAD_DIM, dtype=_F32), jnp.ones((HEAD_DIM, HEAD_DIM), _F32)).astype(_BF)
    xs = (ctx, x)

    for l in range(depth):
        last = l == depth - 1
        skip = nct if last else 0
        lam_init = 0.8 - 0.6 * math.exp(-0.3 * l)
        mod3 = mod[l].reshape(MOD_ROWS, 1, 6 * d)
        w_proj = w_in[l, :, :N_PROJ].astype(_BF)
        w_gate = w_in[l, :, N_PROJ:].astype(_BF)
        gq = jnp.tile(q_norm_g[l], LANES // HEAD_DIM).reshape(1, LANES)
        gk = jnp.tile(k_norm_g[l], LANES // HEAD_DIM).reshape(1, LANES)
        z0, gz1, qp, kn, vt, dqp, dkn, dvt, stats = _inproj(
            xs, mod3, norm1_g[l].reshape(1, d), w_proj, gq, gk, cos, sin, bd, nct, t)

        gates = lambda dr: (_block_diag(w_rg[l, dr]).astype(_BF), b_rg[l, dr].reshape(1, -1),
                            _block_diag(w_ig[l, dr]).astype(_BF), b_ig[l, dr].reshape(1, -1),
                            lru_lambda[l, dr].reshape(1, -1))
        hf, u_conv = _lru_sweep(z0, gates(0), nct, False, conv=(conv_w[l], conv_b[l].reshape(1, -1)))
        y_rec = _lru_sweep(u_conv, gates(1), nct, True, hf=hf, gz1=gz1)

        st = jnp.max(stats, axis=(0, 1, 3))
        gqa_bound = math.sqrt(HEAD_DIM) * jnp.max(jnp.abs(q_norm_g[l])) * jnp.max(jnp.abs(k_norm_g[l]))
        gqa_ok = jnp.logical_and(gqa_bound * BOUND_SLACK <= SAFE_LOGIT, st[2] <= SAFE_VALUE)
        diff_ok = jnp.logical_and(st[0] * st[1] * BOUND_SLACK <= SAFE_LOGIT * SAFE_LOGIT, st[3] <= SAFE_VALUE)
        y_gqa = _gqa_attention(gqa_ok.astype(jnp.int32).reshape(1), qp, kn, vt, nct=nct, tq=tq_gqa, tk=tk)
        vec = lambda a: a[l].reshape(1, HEAD_DIM)
        y_diff = _diff_attention(diff_ok.astype(jnp.int32).reshape(1), dqp, dkn, dvt, vec(lambda_q1), vec(lambda_k1),
                                 vec(lambda_q2), vec(lambda_k2), subln_g[l].reshape(DIFF_V_DIM, 1),
                                 nct=nct, tq=tq_diff, tk=tk, lam_init=lam_init)

        xm = _merge(xs, mod3, norm1_g[l].reshape(1, d), w_gate, b_gate[l].reshape(1, -1), y_rec, y_gqa, y_diff,
                    w_branch[l].astype(_BF), w_out[l].astype(_BF), nct, skip)
        xs = (_mlp(xm, mod3, norm2_g[l].reshape(1, d), w_up[l].astype(_BF), w_down[l].astype(_BF),
                   final_g.reshape(1, d), nct, skip, last),)
    return xs[0]


def kernel(x, c, ctx, c_ctx, w_mod, b_mod, norm1_g, w_in, b_gate, conv_w, conv_b, w_rg, b_rg, w_ig, b_ig, lru_lambda, q_norm_g, k_norm_g, lambda_q1, lambda_k1, lambda_q2, lambda_k2, subln_g, w_branch, w_out, norm2_g, w_up, w_down, final_g):
    return _forward(x, c, ctx, c_ctx, w_mod, b_mod, norm1_g, w_in, b_gate, conv_w, conv_b, w_rg, b_rg, w_ig, b_ig,
                    lru_lambda, q_norm_g, k_norm_g, lambda_q1, lambda_k1, lambda_q2, lambda_k2, subln_g, w_branch,
                    w_out, norm2_g, w_up, w_down, final_g, tq_gqa=256, tq_diff=256, tk=2048)
```

```python
import functools
import math

import jax
import jax.numpy as jnp
from jax import lax
from jax.experimental import pallas as pl
from jax.experimental.pallas import tpu as pltpu

GRID_W = 64
HEAD_DIM = 64
LRU_WIDTH = 512
LRU_BLOCKS = 8
CONV_W = 4
CONV_LEFT = 2
LRU_C = 8.0
GQA_HEADS = 8
GQA_KV_HEADS = 2
GQA_GROUP = GQA_HEADS // GQA_KV_HEADS
DIFF_HEADS = 4
DIFF_V_DIM = 2 * HEAD_DIM
N_BRANCH = 3
ROPE_THETA = 10000.0
EPS = 1e-6
SCALE = HEAD_DIM ** -0.5

TM = 256
HALO = 8
LANES = 128
MOD_ROWS = 8
ONES_ROWS = 16
VMEM_LIMIT = 56 * 1024 * 1024

SEC_LRU = 2 * LRU_WIDTH
SEC_Q = GQA_HEADS * HEAD_DIM
SEC_KV = 2 * GQA_KV_HEADS * HEAD_DIM
SEC_DQ = DIFF_HEADS * 2 * HEAD_DIM
SEC_DV = DIFF_HEADS * DIFF_V_DIM
N_PROJ = SEC_LRU + SEC_Q + SEC_KV + 2 * SEC_DQ + SEC_DV

SAFE_LOGIT = 60.0
SAFE_VALUE = 1.0e6
BOUND_SLACK = 1.05

_BF = jnp.bfloat16
_F32 = jnp.float32


def _params(sem):
    return pltpu.CompilerParams(dimension_semantics=sem, vmem_limit_bytes=VMEM_LIMIT)


def _resident(shape, index_map):
    return pl.BlockSpec(shape, index_map, pipeline_mode=pl.Buffered(1))


def _split_bf16(a):
    hi = a.astype(_BF)
    lo = (a - hi.astype(_F32)).astype(_BF)
    return hi, lo


def _dot(a, b):
    return jnp.dot(a, b, preferred_element_type=_F32)


def _sigmoid(x):
    return 0.5 * jnp.tanh(0.5 * x) + 0.5


def _mod_kernel(c_ref, w_ref, b_ref, o_ref):
    c = c_ref[...]
    s = c * _sigmoid(c)
    s_hi, s_lo = _split_bf16(s)
    w_hi, w_lo = _split_bf16(w_ref[...])
    o_ref[...] = _dot(s_hi, w_hi) + (_dot(s_lo, w_hi) + _dot(s_hi, w_lo)) + b_ref[...]


def _modulation(cin, w_mod, b_mod):
    depth, d, n = w_mod.shape
    tn = 1536
    return pl.pallas_call(
        _mod_kernel,
        grid=(depth, n // tn),
        in_specs=[
            pl.BlockSpec((MOD_ROWS, d), lambda l, j: (0, 0)),
            pl.BlockSpec((None, d, tn), lambda l, j: (l, 0, j)),
            pl.BlockSpec((None, 1, tn), lambda l, j: (l, 0, j)),
        ],
        out_specs=pl.BlockSpec((None, MOD_ROWS, tn), lambda l, j: (l, 0, j)),
        out_shape=jax.ShapeDtypeStruct((depth, MOD_ROWS, n), _F32),
        compiler_params=_params(("arbitrary", "arbitrary")),
        name="modulation",
    )(cin, w_mod, b_mod.reshape(depth, 1, n))


def _rms_mod(x, g, scale, shift):
    y = x * lax.rsqrt(jnp.mean(x * x, axis=-1, keepdims=True) + EPS) * g
    return y * (1.0 + scale) + shift


def _swap16(y):
    lane = lax.broadcasted_iota(jnp.int32, y.shape, 1)
    return jnp.where(lane % 32 < 16, pltpu.roll(y, LANES - 16, 1), pltpu.roll(y, 16, 1))


def _rope(y, cos, sin):
    return y * cos + _swap16(y) * sin


def _chunks(z):
    return [z[:, j * LANES:(j + 1) * LANES] for j in range(z.shape[-1] // LANES)]


def _mod_row(nct, skip):
    def row(b, i, nb):
        return jnp.where(i + skip < nct, nb, b)
    return row


def _stream_specs(xs, skip, nct):
    d = xs[0].shape[-1]
    if len(xs) == 1:
        return [pl.BlockSpec((1, TM, d), lambda b, i: (b, i + skip, 0))]
    return [pl.BlockSpec((1, TM, d), lambda b, i: (b, jnp.minimum(i + skip, nct - 1), 0)),
            pl.BlockSpec((1, TM, d), lambda b, i: (b, jnp.maximum(i + skip - nct, 0), 0))]


def _stream_tile(refs, tile, nct):
    if len(refs) == 1:
        return refs[0][0]
    return jnp.where(tile < nct, refs[0][0], refs[1][0])


def _inproj_kernel(*refs, n_x, nct):
    x_refs = refs[:n_x]
    (mod_ref, g1_ref, w_ref, gq_ref, gk_ref, cos_ref, sin_ref, bd_ref,
     z0_ref, gz1_ref, qp_ref, kn_ref, vt_ref, dqp_ref, dkn_ref, dvt_ref, st_ref) = refs[n_x:]
    x = _stream_tile(x_refs, pl.program_id(1), nct)
    d = x.shape[-1]
    hn = _rms_mod(x, g1_ref[...], mod_ref[0, :, d:2 * d], mod_ref[0, :, 0:d]).astype(_BF)
    cos = cos_ref[...]
    sin = sin_ref[...]
    bd = bd_ref[...]
    lane = lax.broadcasted_iota(jnp.int32, (TM, LANES), 1)
    lo_half = lane < HEAD_DIM

    def proj(a, width):
        return _dot(hn, w_ref[:, a:a + width])

    def head_sumsq(z, exact):
        parts = []
        for h in range(z.shape[-1] // (2 * LANES)):
            sq = z[:, h * 2 * LANES:(h + 1) * 2 * LANES]
            sq = sq * sq
            if exact:
                hi, lo = _split_bf16(sq)
                parts.append(_dot(hi, bd) + _dot(lo, bd))
            else:
                parts.append(_dot(sq.astype(_BF), bd))
        return parts[0] if len(parts) == 1 else jnp.concatenate(parts, axis=-1)

    def amax(a):
        return jnp.max(jnp.max(a, axis=0, keepdims=True), axis=1, keepdims=True)

    c0 = SEC_LRU

    zq = proj(c0, SEC_Q)
    c0 += SEC_Q
    inv = lax.rsqrt(head_sumsq(zq, True) * (1.0 / HEAD_DIM) + EPS)
    for j, (c, r) in enumerate(zip(_chunks(zq), _chunks(inv))):
        val = _rope(c * r * gq_ref[...], cos, sin) * SCALE
        swapped = pltpu.roll(val, HEAD_DIM, 1)
        if (2 * j) // GQA_GROUP == 0:
            slots = (jnp.where(lo_half, val, 0.0), jnp.where(lo_half, swapped, 0.0))
        else:
            slots = (jnp.where(lo_half, 0.0, swapped), jnp.where(lo_half, 0.0, val))
        for n, slot in enumerate(slots):
            qp_ref[0, :, (2 * j + n) * LANES:(2 * j + n + 1) * LANES] = slot.astype(_BF)

    zkv = proj(c0, SEC_KV)
    c0 += SEC_KV
    inv = lax.rsqrt(head_sumsq(zkv, True)[:, :LANES] * (1.0 / HEAD_DIM) + EPS)
    kn_ref[0] = _rope(zkv[:, :LANES] * inv * gk_ref[...], cos, sin).astype(_BF)
    v = zkv[:, LANES:]
    ones_rows = (lax.broadcasted_iota(jnp.int32, (ONES_ROWS, TM), 0) == 0).astype(_BF)
    vt = v.T.astype(_BF)
    for g in range(GQA_KV_HEADS):
        vt_ref[0, g, 0:HEAD_DIM] = vt[g * HEAD_DIM:(g + 1) * HEAD_DIM]
        vt_ref[0, g, HEAD_DIM:HEAD_DIM + ONES_ROWS] = ones_rows

    zdq = proj(c0, SEC_DQ)
    c0 += SEC_DQ
    for n, c in enumerate(_chunks(zdq)):
        val = _rope(c, cos, sin) * SCALE
        dqp_ref[0, :, 2 * n * LANES:(2 * n + 1) * LANES] = jnp.where(lo_half, val, 0.0).astype(_BF)
        dqp_ref[0, :, (2 * n + 1) * LANES:(2 * n + 2) * LANES] = jnp.where(lo_half, 0.0, val).astype(_BF)
    zdk = proj(c0, SEC_DQ)
    c0 += SEC_DQ
    for n, c in enumerate(_chunks(zdk)):
        dkn_ref[0, :, n * LANES:(n + 1) * LANES] = _rope(c, cos, sin).astype(_BF)
    zdv = proj(c0, SEC_DV)
    for n, c in enumerate(_chunks(zdv)):
        dvt_ref[0, n, 0:DIFF_V_DIM] = c.T.astype(_BF)
        dvt_ref[0, n, DIFF_V_DIM:DIFF_V_DIM + ONES_ROWS] = ones_rows

    stats = (amax(head_sumsq(zdq, False)) * (SCALE * SCALE), amax(head_sumsq(zdk, False)),
             amax(jnp.abs(v)), amax(jnp.abs(zdv)))
    row = lax.broadcasted_iota(jnp.int32, (8, LANES), 0)
    st = jnp.zeros((8, LANES), _F32)
    for n, val in enumerate(stats):
        st = jnp.where(row == n, val, st)
    st_ref[0, 0] = st

    gz1_ref[0] = jax.nn.gelu(proj(LRU_WIDTH, LRU_WIDTH))
    z0_ref[0] = proj(0, LRU_WIDTH)


def _inproj(xs, mod3, g1, w_proj, gq, gk, cos, sin, bd, nct, t):
    bsz = xs[0].shape[0]
    d = xs[0].shape[-1]
    nt = t // TM
    row = _mod_row(nct, 0)
    tok = lambda w: pl.BlockSpec((1, TM, w), lambda b, i: (b, i, 0))
    const = lambda shape: pl.BlockSpec(shape, lambda b, i: (0,) * len(shape))
    out_specs = [
        tok(LRU_WIDTH), tok(LRU_WIDTH), tok(GQA_HEADS * LANES),
        tok(LANES),
        pl.BlockSpec((1, GQA_KV_HEADS, HEAD_DIM + ONES_ROWS, TM), lambda b, i: (b, 0, 0, i)),
        tok(2 * DIFF_HEADS * LANES),
        tok(DIFF_HEADS * LANES),
        pl.BlockSpec((1, DIFF_HEADS, DIFF_V_DIM + ONES_ROWS, TM), lambda b, i: (b, 0, 0, i)),
        pl.BlockSpec((1, 1, 8, LANES), lambda b, i: (b, i, 0, 0)),
    ]
    out_shape = [
        jax.ShapeDtypeStruct((bsz, t, LRU_WIDTH), _F32), jax.ShapeDtypeStruct((bsz, t, LRU_WIDTH), _F32),
        jax.ShapeDtypeStruct((bsz, t, GQA_HEADS * LANES), _BF),
        jax.ShapeDtypeStruct((bsz, t, LANES), _BF),
        jax.ShapeDtypeStruct((bsz, GQA_KV_HEADS, HEAD_DIM + ONES_ROWS, t), _BF),
        jax.ShapeDtypeStruct((bsz, t, 2 * DIFF_HEADS * LANES), _BF),
        jax.ShapeDtypeStruct((bsz, t, DIFF_HEADS * LANES), _BF),
        jax.ShapeDtypeStruct((bsz, DIFF_HEADS, DIFF_V_DIM + ONES_ROWS, t), _BF),
        jax.ShapeDtypeStruct((bsz, nt, 8, LANES), _F32),
    ]
    return pl.pallas_call(
        functools.partial(_inproj_kernel, n_x=len(xs), nct=nct),
        grid=(bsz, nt),
        in_specs=_stream_specs(xs, 0, nct) + [
            pl.BlockSpec((1, 1, 2 * d), lambda b, i: (row(b, i, bsz), 0, 0)),
            const((1, d)),
            _resident((d, N_PROJ), lambda b, i: (0, 0)),
            const((1, LANES)), const((1, LANES)),
            pl.BlockSpec((TM, LANES), lambda b, i: (i, 0)),
            pl.BlockSpec((TM, LANES), lambda b, i: (i, 0)),
            const((2 * LANES, 2 * LANES)),
        ],
        out_specs=out_specs,
        out_shape=out_shape,
        compiler_params=_params(("arbitrary", "arbitrary")),
        name="inproj",
    )(*xs, mod3, g1, w_proj, gq, gk, cos, sin, bd)


def _lru_kernel(*refs, nct, nt, reverse):
    if reverse:
        y_ref, wr_ref, br_ref, wi_ref, bi_ref, lam_ref, hf_ref, gz1_ref, o_ref, a_ref, b_ref, h_ref = refs
    else:
        (z_ref, zp_ref, zn_ref, cw_ref, cb_ref, wr_ref, br_ref, wi_ref, bi_ref, lam_ref,
         o_ref, y_ref, ext_ref, a_ref, b_ref, h_ref) = refs
    i = pl.program_id(1)
    tile = _lru_tile(i, nct, nt, reverse)

    @pl.when(i == 0)
    def _():
        h_ref[...] = jnp.zeros_like(h_ref)

    if reverse:
        y = y_ref[0]
    else:
        seg_first = jnp.logical_or(tile == 0, tile == nct)
        seg_last = jnp.logical_or(tile == nct - 1, tile == nt - 1)
        u = z_ref[0]
        ext_ref[0:HALO, :] = jnp.where(seg_first, 0.0, zp_ref[0])
        ext_ref[HALO:HALO + TM, :] = u
        ext_ref[HALO + TM:, :] = jnp.where(seg_last, 0.0, zn_ref[0])
        y = cb_ref[...] + cw_ref[CONV_LEFT:CONV_LEFT + 1, :] * u
        for j in range(CONV_W):
            if j != CONV_LEFT:
                y = y + cw_ref[j:j + 1, :] * ext_ref[pl.ds(HALO + j - CONV_LEFT, TM), :]
        y_ref[0] = y

    yb = y.astype(_BF)
    r = _sigmoid(_dot(yb, wr_ref[...]) + br_ref[...])
    ig = _sigmoid(_dot(yb, wi_ref[...]) + bi_ref[...])
    log_a = r * ((-LRU_C) * jax.nn.softplus(-lam_ref[...]))
    a = jnp.exp(log_a)
    a_ref[...] = a
    b_ref[...] = jnp.sqrt(-jnp.tanh(log_a) * (a * a + 1.0)) * (ig * y)

    ng = TM // 8
    row = lax.broadcasted_iota(jnp.int32, (8, LRU_WIDTH), 0)

    def body(g, h):
        r0 = pl.multiple_of((ng - 1 - g if reverse else g) * 8, 8)
        aa = a_ref[pl.ds(r0, 8), :]
        bb = b_ref[pl.ds(r0, 8), :]
        for s in (1, 2, 4):
            sh = 8 - s if reverse else s
            m = (row < 8 - s) if reverse else (row >= s)
            bb = jnp.where(m, aa * pltpu.roll(bb, sh, 0) + bb, bb)
            aa = jnp.where(m, aa * pltpu.roll(aa, sh, 0), aa)
        hh = aa * h + bb
        if reverse:
            hsum = hf_ref[0, pl.ds(r0, 8), :] + hh
            o_ref[0, pl.ds(r0, 8), :] = (hsum * gz1_ref[0, pl.ds(r0, 8), :]).astype(o_ref.dtype)
            return hh[0:1, :]
        o_ref[0, pl.ds(r0, 8), :] = hh
        return hh[7:8, :]

    h_ref[0:1, :] = lax.fori_loop(0, ng, body, h_ref[0:1, :], unroll=4)


def _lru_tile(i, nct, nt, reverse):
    if not reverse:
        return i
    return jnp.where(i < nct, nct - 1 - i, nt - 1 - (i - nct))


def _lru_sweep(src, gates, nct, reverse, conv=None, hf=None, gz1=None):
    bsz, t, w = src.shape
    nt = t // TM
    hb = TM // HALO
    tile = lambda i: _lru_tile(i, nct, nt, reverse)
    tok = pl.BlockSpec((1, TM, w), lambda b, i: (b, tile(i), 0))
    const = lambda shape: pl.BlockSpec(shape, lambda b, i: (0,) * len(shape))
    gate_specs = [const((w, w)), const((1, w)), const((w, w)), const((1, w)), const((1, w))]
    scratch = [pltpu.VMEM((TM, w), _F32), pltpu.VMEM((TM, w), _F32), pltpu.VMEM((8, w), _F32)]
    if reverse:
        in_specs = [tok] + gate_specs + [tok, tok]
        args = [src, *gates, hf, gz1]
        out_specs, out_shape = tok, jax.ShapeDtypeStruct((bsz, t, w), _BF)
    else:
        in_specs = [
            tok,
            pl.BlockSpec((1, HALO, w), lambda b, i: (b, jnp.maximum(tile(i) * hb - 1, 0), 0)),
            pl.BlockSpec((1, HALO, w), lambda b, i: (b, jnp.minimum((tile(i) + 1) * hb, nt * hb - 1), 0)),
            const((CONV_W, w)), const((1, w)),
        ] + gate_specs
        args = [src, src, src, *conv, *gates]
        out_specs = [tok, tok]
        out_shape = [jax.ShapeDtypeStruct((bsz, t, w), _F32)] * 2
        scratch = [pltpu.VMEM((TM + 2 * HALO, w), _F32)] + scratch
    return pl.pallas_call(
        functools.partial(_lru_kernel, nct=nct, nt=nt, reverse=reverse),
        grid=(bsz, nt),
        in_specs=in_specs,
        out_specs=out_specs,
        out_shape=out_shape,
        scratch_shapes=scratch,
        compiler_params=_params(("arbitrary", "arbitrary")),
        name="lru_bwd" if reverse else "lru_fwd",
    )(*args)


def _attend(qt, k_at, vt_at, v_rows, is_latent, n_ctx, n_lat, tk, stabilize):
    nq = qt.shape[1]

    def step(carry, start, size):
        st = _dot(k_at(start, size), qt)
        if stabilize:
            m, acc = carry
            m_new = jnp.maximum(m, jnp.max(st, axis=0, keepdims=True))
            acc = jnp.exp(m - m_new) * acc + _dot(vt_at(start, size), jnp.exp(st - m_new).astype(_BF))
            return m_new, acc
        return (carry[0] + _dot(vt_at(start, size), jnp.exp(st).astype(_BF)),)

    zero = jnp.zeros((v_rows, nq), _F32)
    init = (jnp.full((1, nq), -jnp.inf, _F32), zero) if stabilize else (zero,)

    def all_keys():
        carry = step(init, 0, n_ctx + tk)
        if stabilize:
            return lax.fori_loop(
                1, n_lat // tk, lambda j, c: step(c, pl.multiple_of(n_ctx + j * tk, math.gcd(n_ctx, tk)), tk), carry)
        for j in range(1, n_lat // tk):
            carry = step(carry, n_ctx + j * tk, tk)
        return carry

    return lax.cond(is_latent, all_keys, lambda: step(init, 0, n_ctx))[-1]


def _attend_guarded(bounded, qt, k_at, vt_at, v_rows, is_latent, n_ctx, n_lat, tk):
    return lax.cond(bounded,
                    lambda: _attend(qt, k_at, vt_at, v_rows, is_latent, n_ctx, n_lat, n_lat // 4, False),
                    lambda: _attend(qt, k_at, vt_at, v_rows, is_latent, n_ctx, n_lat, tk, True))


def _query_slots_t(q_ref):
    return jnp.concatenate([c.astype(_F32).T.astype(_BF) for c in _chunks(q_ref[0])], axis=1)


def _gqa_kernel(flag_ref, q_ref, k_ref, vt_ref, o_ref, *, nct, tk):
    tq = q_ref.shape[1]
    n_ctx = nct * TM
    acc = _attend_guarded(
        flag_ref[0] != 0, _query_slots_t(q_ref), lambda a, n: k_ref[0, pl.ds(a, n), :],
        lambda a, n: vt_ref[0, 0, :, pl.ds(a, n)], vt_ref.shape[2],
        pl.program_id(2) >= n_ctx // tq, n_ctx, k_ref.shape[1] - n_ctx, tk)
    ot = acc[:HEAD_DIM] / acc[HEAD_DIM:HEAD_DIM + 1]
    pairs = [jnp.concatenate([ot[:, 2 * p * tq:(2 * p + 1) * tq], ot[:, (2 * p + 1) * tq:(2 * p + 2) * tq]], axis=0).T
             for p in range(GQA_GROUP // 2)]
    o_ref[0] = jnp.concatenate(pairs, axis=-1).astype(o_ref.dtype)


def _gqa_attention(flag, qp, kn, vt, *, nct, tq, tk):
    bsz, t, _ = qp.shape
    gw = GQA_GROUP * LANES
    return pl.pallas_call(
        functools.partial(_gqa_kernel, nct=nct, tk=tk),
        grid_spec=pltpu.PrefetchScalarGridSpec(
            num_scalar_prefetch=1,
            grid=(bsz, GQA_KV_HEADS, t // tq),
            in_specs=[
                pl.BlockSpec((1, tq, gw), lambda b, g, i, f: (b, i, g)),
                pl.BlockSpec((1, t, LANES), lambda b, g, i, f: (b, 0, 0)),
                pl.BlockSpec((1, 1, vt.shape[2], t), lambda b, g, i, f: (b, g, 0, 0)),
            ],
            out_specs=pl.BlockSpec((1, tq, GQA_GROUP * HEAD_DIM), lambda b, g, i, f: (b, i, g)),
        ),
        out_shape=jax.ShapeDtypeStruct((bsz, t, GQA_HEADS * HEAD_DIM), _BF),
        compiler_params=_params(("arbitrary", "arbitrary", "arbitrary")),
        name="gqa_attention",
    )(flag, qp, kn, vt)


def _diff_kernel(flag_ref, q_ref, k_ref, vt_ref, lq1_ref, lk1_ref, lq2_ref, lk2_ref, g_ref, o_ref, *,
                 nct, tk, lam_init):
    tq = q_ref.shape[1]
    n_ctx = nct * TM
    acc = _attend_guarded(
        flag_ref[0] != 0, _query_slots_t(q_ref), lambda a, n: k_ref[0, pl.ds(a, n), :],
        lambda a, n: vt_ref[0, 0, :, pl.ds(a, n)], vt_ref.shape[2],
        pl.program_id(2) >= n_ctx // tq, n_ctx, k_ref.shape[1] - n_ctx, tk)
    lam = (jnp.exp(jnp.sum(lq1_ref[...] * lk1_ref[...], axis=-1, keepdims=True))
           - jnp.exp(jnp.sum(lq2_ref[...] * lk2_ref[...], axis=-1, keepdims=True)) + lam_init)
    a1 = acc[:, :tq]
    a2 = acc[:, tq:]
    ot = (a1[:DIFF_V_DIM] / a1[DIFF_V_DIM:DIFF_V_DIM + 1]
          - lam * (a2[:DIFF_V_DIM] / a2[DIFF_V_DIM:DIFF_V_DIM + 1]))
    ot = ot * lax.rsqrt(jnp.mean(ot * ot, axis=0, keepdims=True) + EPS) * g_ref[...]
    o_ref[0] = (ot * (1.0 - lam_init)).T.astype(o_ref.dtype)


def _diff_attention(flag, dqp, dkn, dvt, lq1, lk1, lq2, lk2, subln_g, *, nct, tq, tk, lam_init):
    bsz, t, _ = dqp.shape
    vec = pl.BlockSpec((1, HEAD_DIM), lambda b, n, i, f: (0, 0))
    return pl.pallas_call(
        functools.partial(_diff_kernel, nct=nct, tk=tk, lam_init=lam_init),
        grid_spec=pltpu.PrefetchScalarGridSpec(
            num_scalar_prefetch=1,
            grid=(bsz, DIFF_HEADS, t // tq),
            in_specs=[
                pl.BlockSpec((1, tq, 2 * LANES), lambda b, n, i, f: (b, i, n)),
                pl.BlockSpec((1, t, LANES), lambda b, n, i, f: (b, 0, n)),
                pl.BlockSpec((1, 1, dvt.shape[2], t), lambda b, n, i, f: (b, n, 0, 0)),
                vec, vec, vec, vec,
                pl.BlockSpec((DIFF_V_DIM, 1), lambda b, n, i, f: (0, 0)),
            ],
            out_specs=pl.BlockSpec((1, tq, DIFF_V_DIM), lambda b, n, i, f: (b, i, n)),
        ),
        out_shape=jax.ShapeDtypeStruct((bsz, t, DIFF_HEADS * DIFF_V_DIM), _BF),
        compiler_params=_params(("arbitrary", "arbitrary", "arbitrary")),
        name="diff_attention",
    )(flag, dqp, dkn, dvt, lq1, lk1, lq2, lk2, subln_g)


def _merge_mlp_kernel(*refs, n_x, nct, skip, final):
    x_refs = refs[:n_x]
    (mod_ref, g1_ref, wg_ref, bg_ref, yr_ref, ya_ref, yd_ref, wb_ref, wo_ref,
     g2_ref, wu_ref, wd_ref, gf_ref, o_ref) = refs[n_x:]
    x = _stream_tile(x_refs, pl.program_id(1) + skip, nct)
    d = x.shape[-1]
    mod = lambda n: mod_ref[0, :, n * d:(n + 1) * d]
    hn = _rms_mod(x, g1_ref[...], mod(1), mod(0)).astype(_BF)
    ys = (yr_ref[0], ya_ref[0], yd_ref[0])
    m = None
    for n in range(N_BRANCH):
        g = _sigmoid(_dot(hn, wg_ref[:, n * d:(n + 1) * d]) + bg_ref[:, n * d:(n + 1) * d])
        term = g * _dot(ys[n], wb_ref[n])
        m = term if m is None else m + term
    x = x + mod(2) * _dot(m.astype(_BF), wo_ref[...])

    h = _rms_mod(x, g2_ref[...], mod(4), mod(3)).astype(_BF)
    acc = None
    for c in range(wu_ref.shape[-1] // d):
        u = jnp.maximum(_dot(h, wu_ref[:, c * d:(c + 1) * d]), 0.0)
        part = _dot((u * u).astype(_BF), wd_ref[c * d:(c + 1) * d, :])
        acc = part if acc is None else acc + part
    y = x + mod(5) * acc
    if final:
        y = y * lax.rsqrt(jnp.mean(y * y, axis=-1, keepdims=True) + EPS) * gf_ref[...]
    o_ref[0] = y


def _merge_mlp(xs, mod3, g1, w_gate, b_gate, y_rec, y_gqa, y_diff, w_branch, w_out, g2, w_up, w_down, final_g,
               nct, skip, final):
    bsz, t, bw = y_rec.shape
    d = xs[0].shape[-1]
    d_ff = w_up.shape[-1]
    nt = t // TM - skip
    row = _mod_row(nct, skip)
    tok = lambda w: pl.BlockSpec((1, TM, w), lambda b, i: (b, i + skip, 0))
    vec = pl.BlockSpec((1, d), lambda b, i: (0, 0))
    return pl.pallas_call(
        functools.partial(_merge_mlp_kernel, n_x=len(xs), nct=nct, skip=skip, final=final),
        grid=(bsz, nt),
        in_specs=_stream_specs(xs, skip, nct) + [
            pl.BlockSpec((1, 1, 6 * d), lambda b, i: (row(b, i, bsz), 0, 0)),
            vec,
            _resident((d, N_BRANCH * d), lambda b, i: (0, 0)),
            pl.BlockSpec((1, N_BRANCH * d), lambda b, i: (0, 0)),
            tok(bw), tok(bw), tok(bw),
            _resident((N_BRANCH, bw, d), lambda b, i: (0, 0, 0)),
            _resident((d, d), lambda b, i: (0, 0)),
            vec,
            _resident((d, d_ff), lambda b, i: (0, 0)),
            _resident((d_ff, d), lambda b, i: (0, 0)),
            vec,
        ],
        out_specs=pl.BlockSpec((1, TM, d), lambda b, i: (b, i, 0)),
        out_shape=jax.ShapeDtypeStruct((bsz, nt * TM, d), _F32),
        compiler_params=_params(("arbitrary", "arbitrary")),
        name="merge_mlp",
    )(*xs, mod3, g1, w_gate, b_gate, y_rec, y_gqa, y_diff, w_branch, w_out, g2, w_up, w_down, final_g)


def _rope_tables(n_ctx, s):
    rows = s // GRID_W
    pos_r = jnp.repeat(jnp.arange(rows, dtype=_F32), GRID_W)
    pos_c = jnp.tile(jnp.arange(GRID_W, dtype=_F32), rows)
    n_freq = HEAD_DIM // 4
    inv = ROPE_THETA ** (-jnp.arange(n_freq, dtype=_F32) * 2.0 / (HEAD_DIM // 2))
    ang_r = pos_r[:, None] * inv
    ang_c = pos_c[:, None] * inv
    ang = jnp.concatenate([ang_r, ang_r, ang_c, ang_c], axis=-1)
    sign = jnp.tile(jnp.concatenate([-jnp.ones(n_freq, _F32), jnp.ones(n_freq, _F32)]), 2)
    cos = jnp.concatenate([jnp.ones((n_ctx, HEAD_DIM), _F32), jnp.cos(ang)], axis=0)
    sin = jnp.concatenate([jnp.zeros((n_ctx, HEAD_DIM), _F32), jnp.sin(ang) * sign], axis=0)
    return jnp.tile(cos, (1, LANES // HEAD_DIM)), jnp.tile(sin, (1, LANES // HEAD_DIM))


def _block_diag(w):
    nb, n, _ = w.shape
    eye = jnp.eye(nb, dtype=w.dtype)
    return jnp.einsum('ncd,nm->ncmd', w, eye).reshape(nb * n, nb * n)


def _forward(x, c, ctx, c_ctx, w_mod, b_mod, norm1_g, w_in, b_gate, conv_w, conv_b, w_rg, b_rg, w_ig, b_ig,
             lru_lambda, q_norm_g, k_norm_g, lambda_q1, lambda_k1, lambda_q2, lambda_k2, subln_g, w_branch,
             w_out, norm2_g, w_up, w_down, final_g, *, tq_gqa, tq_diff, tk):
    bsz, s, d = x.shape
    n_ctx = ctx.shape[1]
    depth = w_mod.shape[0]
    assert n_ctx % TM == 0 and n_ctx > 0 and s % TM == 0 and s % tk == 0 and bsz < MOD_ROWS
    assert TM % tq_gqa == 0 and TM % tq_diff == 0
    nct = n_ctx // TM
    t = n_ctx + s

    cin = jnp.zeros((MOD_ROWS, d), _F32).at[:bsz].set(c).at[bsz].set(c_ctx)
    mod = _modulation(cin, w_mod, b_mod)
    cos, sin = _rope_tables(n_ctx, s)
    bd = jnp.kron(jnp.eye(2 * LANES // HEAD_DIM, dtype=_F32), jnp.ones((HEAD_DIM, HEAD_DIM), _F32)).astype(_BF)
    xs = (ctx, x)

    for l in range(depth):
        last = l == depth - 1
        skip = nct if last else 0
        lam_init = 0.8 - 0.6 * math.exp(-0.3 * l)
        mod3 = mod[l].reshape(MOD_ROWS, 1, 6 * d)
        w_proj = w_in[l, :, :N_PROJ].astype(_BF)
        w_gate = w_in[l, :, N_PROJ:].astype(_BF)
        gq = jnp.tile(q_norm_g[l], LANES // HEAD_DIM).reshape(1, LANES)
        gk = jnp.tile(k_norm_g[l], LANES // HEAD_DIM).reshape(1, LANES)
        z0, gz1, qp, kn, vt, dqp, dkn, dvt, stats = _inproj(
            xs, mod3, norm1_g[l].reshape(1, d), w_proj, gq, gk, cos, sin, bd, nct, t)

        gates = lambda dr: (_block_diag(w_rg[l, dr]).astype(_BF), b_rg[l, dr].reshape(1, -1),
                            _block_diag(w_ig[l, dr]).astype(_BF), b_ig[l, dr].reshape(1, -1),
                            lru_lambda[l, dr].reshape(1, -1))
        hf, u_conv = _lru_sweep(z0, gates(0), nct, False, conv=(conv_w[l], conv_b[l].reshape(1, -1)))
        y_rec = _lru_sweep(u_conv, gates(1), nct, True, hf=hf, gz1=gz1)

        st = jnp.max(stats, axis=(0, 1, 3))
        gqa_bound = math.sqrt(HEAD_DIM) * jnp.max(jnp.abs(q_norm_g[l])) * jnp.max(jnp.abs(k_norm_g[l]))
        gqa_ok = jnp.logical_and(gqa_bound * BOUND_SLACK <= SAFE_LOGIT, st[2] <= SAFE_VALUE)
        diff_ok = jnp.logical_and(st[0] * st[1] * BOUND_SLACK <= SAFE_LOGIT * SAFE_LOGIT, st[3] <= SAFE_VALUE)
        y_gqa = _gqa_attention(gqa_ok.astype(jnp.int32).reshape(1), qp, kn, vt, nct=nct, tq=tq_gqa, tk=tk)
        vec = lambda a: a[l].reshape(1, HEAD_DIM)
        y_diff = _diff_attention(diff_ok.astype(jnp.int32).reshape(1), dqp, dkn, dvt, vec(lambda_q1), vec(lambda_k1),
                                 vec(lambda_q2), vec(lambda_k2), subln_g[l].reshape(DIFF_V_DIM, 1),
                                 nct=nct, tq=tq_diff, tk=tk, lam_init=lam_init)

        xs = (_merge_mlp(xs, mod3, norm1_g[l].reshape(1, d), w_gate, b_gate[l].reshape(1, -1), y_rec, y_gqa, y_diff,
                         w_branch[l].astype(_BF), w_out[l].astype(_BF), norm2_g[l].reshape(1, d),
                         w_up[l].astype(_BF), w_down[l].astype(_BF), final_g.reshape(1, d), nct, skip, last),)
    return xs[0]


def kernel(x, c, ctx, c_ctx, w_mod, b_mod, norm1_g, w_in, b_gate, conv_w, conv_b, w_rg, b_rg, w_ig, b_ig, lru_lambda, q_norm_g, k_norm_g, lambda_q1, lambda_k1, lambda_q2, lambda_k2, subln_g, w_branch, w_out, norm2_g, w_up, w_down, final_g):
    return _forward(x, c, ctx, c_ctx, w_mod, b_mod, norm1_g, w_in, b_gate, conv_w, conv_b, w_rg, b_rg, w_ig, b_ig,
                    lru_lambda, q_norm_g, k_norm_g, lambda_q1, lambda_k1, lambda_q2, lambda_k2, subln_g, w_branch,
                    w_out, norm2_g, w_up, w_down, final_g, tq_gqa=256, tq_diff=256, tk=2048)
```

```python
import functools
import math

import jax
import jax.numpy as jnp
from jax import lax
from jax.experimental import pallas as pl
from jax.experimental.pallas import tpu as pltpu

GRID_W = 64
HEAD_DIM = 64
LRU_WIDTH = 512
LRU_BLOCKS = 8
CONV_W = 4
CONV_LEFT = 2
LRU_C = 8.0
GQA_HEADS = 8
GQA_KV_HEADS = 2
GQA_GROUP = GQA_HEADS // GQA_KV_HEADS
DIFF_HEADS = 4
DIFF_V_DIM = 2 * HEAD_DIM
N_BRANCH = 3
ROPE_THETA = 10000.0
EPS = 1e-6
SCALE = HEAD_DIM ** -0.5

TM = 256
HALO = 8
LANES = 128
MOD_ROWS = 8
ONES_ROWS = 16
VMEM_LIMIT = 56 * 1024 * 1024

SEC_LRU = 2 * LRU_WIDTH
SEC_Q = GQA_HEADS * HEAD_DIM
SEC_KV = 2 * GQA_KV_HEADS * HEAD_DIM
SEC_DQ = DIFF_HEADS * 2 * HEAD_DIM
SEC_DV = DIFF_HEADS * DIFF_V_DIM
N_PROJ = SEC_LRU + SEC_Q + SEC_KV + 2 * SEC_DQ + SEC_DV

SAFE_LOGIT = 60.0
SAFE_VALUE = 1.0e6
BOUND_SLACK = 1.05

_BF = jnp.bfloat16
_F32 = jnp.float32


def _params(sem):
    return pltpu.CompilerParams(dimension_semantics=sem, vmem_limit_bytes=VMEM_LIMIT)


def _resident(shape, index_map):
    return pl.BlockSpec(shape, index_map, pipeline_mode=pl.Buffered(1))


def _split_bf16(a):
    hi = a.astype(_BF)
    lo = (a - hi.astype(_F32)).astype(_BF)
    return hi, lo


def _dot(a, b):
    return jnp.dot(a, b, preferred_element_type=_F32)


def _sigmoid(x):
    return 0.5 * jnp.tanh(0.5 * x) + 0.5


def _mod_kernel(c_ref, w_ref, b_ref, o_ref):
    c = c_ref[...]
    s = c * _sigmoid(c)
    s_hi, s_lo = _split_bf16(s)
    w_hi, w_lo = _split_bf16(w_ref[...])
    o_ref[...] = _dot(s_hi, w_hi) + (_dot(s_lo, w_hi) + _dot(s_hi, w_lo)) + b_ref[...]


def _modulation(cin, w_mod, b_mod):
    depth, d, n = w_mod.shape
    tn = 1536
    return pl.pallas_call(
        _mod_kernel,
        grid=(depth, n // tn),
        in_specs=[
            pl.BlockSpec((MOD_ROWS, d), lambda l, j: (0, 0)),
            pl.BlockSpec((None, d, tn), lambda l, j: (l, 0, j)),
            pl.BlockSpec((None, 1, tn), lambda l, j: (l, 0, j)),
        ],
        out_specs=pl.BlockSpec((None, MOD_ROWS, tn), lambda l, j: (l, 0, j)),
        out_shape=jax.ShapeDtypeStruct((depth, MOD_ROWS, n), _F32),
        compiler_params=_params(("arbitrary", "arbitrary")),
        name="modulation",
    )(cin, w_mod, b_mod.reshape(depth, 1, n))


def _rms_mod(x, g, scale, shift):
    y = x * lax.rsqrt(jnp.mean(x * x, axis=-1, keepdims=True) + EPS) * g
    return y * (1.0 + scale) + shift


def _swap16(y):
    lane = lax.broadcasted_iota(jnp.int32, y.shape, 1)
    return jnp.where(lane % 32 < 16, pltpu.roll(y, LANES - 16, 1), pltpu.roll(y, 16, 1))


def _rope(y, cos, sin):
    return y * cos + _swap16(y) * sin


def _chunks(z):
    return [z[:, j * LANES:(j + 1) * LANES] for j in range(z.shape[-1] // LANES)]


def _mod_row(nct, skip):
    def row(b, i, nb):
        return jnp.where(i + skip < nct, nb, b)
    return row


def _stream_specs(xs, skip, nct):
    d = xs[0].shape[-1]
    if len(xs) == 1:
        return [pl.BlockSpec((1, TM, d), lambda b, i: (b, i + skip, 0))]
    return [pl.BlockSpec((1, TM, d), lambda b, i: (b, jnp.minimum(i + skip, nct - 1), 0)),
            pl.BlockSpec((1, TM, d), lambda b, i: (b, jnp.maximum(i + skip - nct, 0), 0))]


def _stream_tile(refs, tile, nct):
    if len(refs) == 1:
        return refs[0][0]
    return jnp.where(tile < nct, refs[0][0], refs[1][0])


def _inproj_kernel(*refs, n_x, nct):
    x_refs = refs[:n_x]
    (mod_ref, g1_ref, w_ref, gq_ref, gk_ref, cos_ref, sin_ref, bd_ref,
     z0_ref, gz1_ref, qp_ref, kn_ref, vt_ref, dqp_ref, dkn_ref, dvt_ref, st_ref) = refs[n_x:]
    x = _stream_tile(x_refs, pl.program_id(1), nct)
    d = x.shape[-1]
    hn = _rms_mod(x, g1_ref[...], mod_ref[0, :, d:2 * d], mod_ref[0, :, 0:d]).astype(_BF)
    cos = cos_ref[...]
    sin = sin_ref[...]
    bd = bd_ref[...]
    lane = lax.broadcasted_iota(jnp.int32, (TM, LANES), 1)
    lo_half = lane < HEAD_DIM

    def proj(a, width):
        return _dot(hn, w_ref[:, a:a + width])

    def head_sumsq(z, exact):
        parts = []
        for h in range(z.shape[-1] // (2 * LANES)):
            sq = z[:, h * 2 * LANES:(h + 1) * 2 * LANES]
            sq = sq * sq
            if exact:
                hi, lo = _split_bf16(sq)
                parts.append(_dot(hi, bd) + _dot(lo, bd))
            else:
                parts.append(_dot(sq.astype(_BF), bd))
        return parts[0] if len(parts) == 1 else jnp.concatenate(parts, axis=-1)

    def amax(a):
        return jnp.max(jnp.max(a, axis=0, keepdims=True), axis=1, keepdims=True)

    c0 = SEC_LRU

    zq = proj(c0, SEC_Q)
    c0 += SEC_Q
    inv = lax.rsqrt(head_sumsq(zq, True) * (1.0 / HEAD_DIM) + EPS)
    for j, (c, r) in enumerate(zip(_chunks(zq), _chunks(inv))):
        val = _rope(c * r * gq_ref[...], cos, sin) * SCALE
        swapped = pltpu.roll(val, HEAD_DIM, 1)
        if (2 * j) // GQA_GROUP == 0:
            slots = (jnp.where(lo_half, val, 0.0), jnp.where(lo_half, swapped, 0.0))
        else:
            slots = (jnp.where(lo_half, 0.0, swapped), jnp.where(lo_half, 0.0, val))
        for n, slot in enumerate(slots):
            qp_ref[0, :, (2 * j + n) * LANES:(2 * j + n + 1) * LANES] = slot.astype(_BF)

    zkv = proj(c0, SEC_KV)
    c0 += SEC_KV
    inv = lax.rsqrt(head_sumsq(zkv, True)[:, :LANES] * (1.0 / HEAD_DIM) + EPS)
    kn_ref[0] = _rope(zkv[:, :LANES] * inv * gk_ref[...], cos, sin).astype(_BF)
    v = zkv[:, LANES:]
    ones_rows = (lax.broadcasted_iota(jnp.int32, (ONES_ROWS, TM), 0) == 0).astype(_BF)
    vt = v.T.astype(_BF)
    for g in range(GQA_KV_HEADS):
        vt_ref[0, g, 0:HEAD_DIM] = vt[g * HEAD_DIM:(g + 1) * HEAD_DIM]
        vt_ref[0, g, HEAD_DIM:HEAD_DIM + ONES_ROWS] = ones_rows

    zdq = proj(c0, SEC_DQ)
    c0 += SEC_DQ
    for n, c in enumerate(_chunks(zdq)):
        val = _rope(c, cos, sin) * SCALE
        dqp_ref[0, :, 2 * n * LANES:(2 * n + 1) * LANES] = jnp.where(lo_half, val, 0.0).astype(_BF)
        dqp_ref[0, :, (2 * n + 1) * LANES:(2 * n + 2) * LANES] = jnp.where(lo_half, 0.0, val).astype(_BF)
    zdk = proj(c0, SEC_DQ)
    c0 += SEC_DQ
    for n, c in enumerate(_chunks(zdk)):
        dkn_ref[0, :, n * LANES:(n + 1) * LANES] = _rope(c, cos, sin).astype(_BF)
    zdv = proj(c0, SEC_DV)
    for n, c in enumerate(_chunks(zdv)):
        dvt_ref[0, n, 0:DIFF_V_DIM] = c.T.astype(_BF)
        dvt_ref[0, n, DIFF_V_DIM:DIFF_V_DIM + ONES_ROWS] = ones_rows

    stats = (amax(head_sumsq(zdq, False)) * (SCALE * SCALE), amax(head_sumsq(zdk, False)),
             amax(jnp.abs(v)), amax(jnp.abs(zdv)))
    row = lax.broadcasted_iota(jnp.int32, (8, LANES), 0)
    st = jnp.zeros((8, LANES), _F32)
    for n, val in enumerate(stats):
        st = jnp.where(row == n, val, st)
    st_ref[0, 0] = st

    gz1_ref[0] = jax.nn.gelu(proj(LRU_WIDTH, LRU_WIDTH))
    z0_ref[0] = proj(0, LRU_WIDTH)


def _inproj(xs, mod3, g1, w_proj, gq, gk, cos, sin, bd, nct, t):
    bsz = xs[0].shape[0]
    d = xs[0].shape[-1]
    nt = t // TM
    row = _mod_row(nct, 0)
    tok = lambda w: pl.BlockSpec((1, TM, w), lambda b, i: (b, i, 0))
    const = lambda shape: pl.BlockSpec(shape, lambda b, i: (0,) * len(shape))
    out_specs = [
        tok(LRU_WIDTH), tok(LRU_WIDTH), tok(GQA_HEADS * LANES),
        tok(LANES),
        pl.BlockSpec((1, GQA_KV_HEADS, HEAD_DIM + ONES_ROWS, TM), lambda b, i: (b, 0, 0, i)),
        tok(2 * DIFF_HEADS * LANES),
        tok(DIFF_HEADS * LANES),
        pl.BlockSpec((1, DIFF_HEADS, DIFF_V_DIM + ONES_ROWS, TM), lambda b, i: (b, 0, 0, i)),
        pl.BlockSpec((1, 1, 8, LANES), lambda b, i: (b, i, 0, 0)),
    ]
    out_shape = [
        jax.ShapeDtypeStruct((bsz, t, LRU_WIDTH), _F32), jax.ShapeDtypeStruct((bsz, t, LRU_WIDTH), _F32),
        jax.ShapeDtypeStruct((bsz, t, GQA_HEADS * LANES), _BF),
        jax.ShapeDtypeStruct((bsz, t, LANES), _BF),
        jax.ShapeDtypeStruct((bsz, GQA_KV_HEADS, HEAD_DIM + ONES_ROWS, t), _BF),
        jax.ShapeDtypeStruct((bsz, t, 2 * DIFF_HEADS * LANES), _BF),
        jax.ShapeDtypeStruct((bsz, t, DIFF_HEADS * LANES), _BF),
        jax.ShapeDtypeStruct((bsz, DIFF_HEADS, DIFF_V_DIM + ONES_ROWS, t), _BF),
        jax.ShapeDtypeStruct((bsz, nt, 8, LANES), _F32),
    ]
    return pl.pallas_call(
        functools.partial(_inproj_kernel, n_x=len(xs), nct=nct),
        grid=(bsz, nt),
        in_specs=_stream_specs(xs, 0, nct) + [
            pl.BlockSpec((1, 1, 2 * d), lambda b, i: (row(b, i, bsz), 0, 0)),
            const((1, d)),
            _resident((d, N_PROJ), lambda b, i: (0, 0)),
            const((1, LANES)), const((1, LANES)),
            pl.BlockSpec((TM, LANES), lambda b, i: (i, 0)),
            pl.BlockSpec((TM, LANES), lambda b, i: (i, 0)),
            const((2 * LANES, 2 * LANES)),
        ],
        out_specs=out_specs,
        out_shape=out_shape,
        compiler_params=_params(("arbitrary", "arbitrary")),
        name="inproj",
    )(*xs, mod3, g1, w_proj, gq, gk, cos, sin, bd)


def _lru_kernel(*refs, nct, nt, reverse):
    if reverse:
        y_ref, wr_ref, br_ref, wi_ref, bi_ref, lam_ref, hf_ref, gz1_ref, o_ref, a_ref, b_ref, h_ref = refs
    else:
        (z_ref, zp_ref, zn_ref, cw_ref, cb_ref, wr_ref, br_ref, wi_ref, bi_ref, lam_ref,
         o_ref, y_ref, ext_ref, a_ref, b_ref, h_ref) = refs
    i = pl.program_id(1)
    tile = _lru_tile(i, nct, nt, reverse)

    @pl.when(i == 0)
    def _():
        h_ref[...] = jnp.zeros_like(h_ref)

    if reverse:
        y = y_ref[0]
    else:
        seg_first = jnp.logical_or(tile == 0, tile == nct)
        seg_last = jnp.logical_or(tile == nct - 1, tile == nt - 1)
        u = z_ref[0]
        ext_ref[0:HALO, :] = jnp.where(seg_first, 0.0, zp_ref[0])
        ext_ref[HALO:HALO + TM, :] = u
        ext_ref[HALO + TM:, :] = jnp.where(seg_last, 0.0, zn_ref[0])
        y = cb_ref[...] + cw_ref[CONV_LEFT:CONV_LEFT + 1, :] * u
        for j in range(CONV_W):
            if j != CONV_LEFT:
                y = y + cw_ref[j:j + 1, :] * ext_ref[pl.ds(HALO + j - CONV_LEFT, TM), :]
        y_ref[0] = y

    yb = y.astype(_BF)
    r = _sigmoid(_dot(yb, wr_ref[...]) + br_ref[...])
    ig = _sigmoid(_dot(yb, wi_ref[...]) + bi_ref[...])
    log_a = r * ((-LRU_C) * jax.nn.softplus(-lam_ref[...]))
    a = jnp.exp(log_a)
    a_ref[...] = a
    b_ref[...] = jnp.sqrt(-jnp.tanh(log_a) * (a * a + 1.0)) * (ig * y)

    ng = TM // 8
    row = lax.broadcasted_iota(jnp.int32, (8, LRU_WIDTH), 0)

    def body(g, h):
        r0 = pl.multiple_of((ng - 1 - g if reverse else g) * 8, 8)
        aa = a_ref[pl.ds(r0, 8), :]
        bb = b_ref[pl.ds(r0, 8), :]
        for s in (1, 2, 4):
            sh = 8 - s if reverse else s
            m = (row < 8 - s) if reverse else (row >= s)
            bb = jnp.where(m, aa * pltpu.roll(bb, sh, 0) + bb, bb)
            aa = jnp.where(m, aa * pltpu.roll(aa, sh, 0), aa)
        hh = aa * h + bb
        if reverse:
            hsum = hf_ref[0, pl.ds(r0, 8), :] + hh
            o_ref[0, pl.ds(r0, 8), :] = (hsum * gz1_ref[0, pl.ds(r0, 8), :]).astype(o_ref.dtype)
            return hh[0:1, :]
        o_ref[0, pl.ds(r0, 8), :] = hh
        return hh[7:8, :]

    h_ref[0:1, :] = lax.fori_loop(0, ng, body, h_ref[0:1, :], unroll=4)


def _lru_tile(i, nct, nt, reverse):
    if not reverse:
        return i
    return jnp.where(i < nct, nct - 1 - i, nt - 1 - (i - nct))


def _lru_sweep(src, gates, nct, reverse, conv=None, hf=None, gz1=None):
    bsz, t, w = src.shape
    nt = t // TM
    hb = TM // HALO
    tile = lambda i: _lru_tile(i, nct, nt, reverse)
    tok = pl.BlockSpec((1, TM, w), lambda b, i: (b, tile(i), 0))
    const = lambda shape: pl.BlockSpec(shape, lambda b, i: (0,) * len(shape))
    gate_specs = [const((w, w)), const((1, w)), const((w, w)), const((1, w)), const((1, w))]
    scratch = [pltpu.VMEM((TM, w), _F32), pltpu.VMEM((TM, w), _F32), pltpu.VMEM((8, w), _F32)]
    if reverse:
        in_specs = [tok] + gate_specs + [tok, tok]
        args = [src, *gates, hf, gz1]
        out_specs, out_shape = tok, jax.ShapeDtypeStruct((bsz, t, w), _BF)
    else:
        in_specs = [
            tok,
            pl.BlockSpec((1, HALO, w), lambda b, i: (b, jnp.maximum(tile(i) * hb - 1, 0), 0)),
            pl.BlockSpec((1, HALO, w), lambda b, i: (b, jnp.minimum((tile(i) + 1) * hb, nt * hb - 1), 0)),
            const((CONV_W, w)), const((1, w)),
        ] + gate_specs
        args = [src, src, src, *conv, *gates]
        out_specs = [tok, tok]
        out_shape = [jax.ShapeDtypeStruct((bsz, t, w), _F32)] * 2
        scratch = [pltpu.VMEM((TM + 2 * HALO, w), _F32)] + scratch
    return pl.pallas_call(
        functools.partial(_lru_kernel, nct=nct, nt=nt, reverse=reverse),
        grid=(bsz, nt),
        in_specs=in_specs,
        out_specs=out_specs,
        out_shape=out_shape,
        scratch_shapes=scratch,
        compiler_params=_params(("arbitrary", "arbitrary")),
        name="lru_bwd" if reverse else "lru_fwd",
    )(*args)


def _attend(qt, k_at, vt_at, v_rows, is_latent, n_ctx, n_lat, tk, stabilize):
    nq = qt.shape[1]

    def step(carry, start, size):
        st = _dot(k_at(start, size), qt)
        if stabilize:
            m, acc = carry
            m_new = jnp.maximum(m, jnp.max(st, axis=0, keepdims=True))
            acc = jnp.exp(m - m_new) * acc + _dot(vt_at(start, size), jnp.exp(st - m_new).astype(_BF))
            return m_new, acc
        p = jnp.exp(st)
        vd = v_rows - ONES_ROWS
        acc = carry[0]
        upd = _dot(vt_at(start, size)[:vd], p.astype(_BF))
        den = jnp.sum(p, axis=0, keepdims=True)
        return (jnp.concatenate([acc[:vd] + upd, acc[vd:vd + 1] + den, acc[vd + 1:]], axis=0),)

    zero = jnp.zeros((v_rows, nq), _F32)
    init = (jnp.full((1, nq), -jnp.inf, _F32), zero) if stabilize else (zero,)

    def all_keys():
        carry = step(init, 0, n_ctx + tk)
        if stabilize:
            return lax.fori_loop(
                1, n_lat // tk, lambda j, c: step(c, pl.multiple_of(n_ctx + j * tk, math.gcd(n_ctx, tk)), tk), carry)
        for j in range(1, n_lat // tk):
            carry = step(carry, n_ctx + j * tk, tk)
        return carry

    return lax.cond(is_latent, all_keys, lambda: step(init, 0, n_ctx))[-1]


def _attend_guarded(bounded, qt, k_at, vt_at, v_rows, is_latent, n_ctx, n_lat, tk):
    return lax.cond(bounded,
                    lambda: _attend(qt, k_at, vt_at, v_rows, is_latent, n_ctx, n_lat, n_lat // 4, False),
                    lambda: _attend(qt, k_at, vt_at, v_rows, is_latent, n_ctx, n_lat, tk, True))


def _query_slots_t(q_ref):
    return jnp.concatenate([c.astype(_F32).T.astype(_BF) for c in _chunks(q_ref[0])], axis=1)


def _gqa_kernel(flag_ref, q_ref, k_ref, vt_ref, o_ref, *, nct, tk):
    tq = q_ref.shape[1]
    n_ctx = nct * TM
    acc = _attend_guarded(
        flag_ref[0] != 0, _query_slots_t(q_ref), lambda a, n: k_ref[0, pl.ds(a, n), :],
        lambda a, n: vt_ref[0, 0, :, pl.ds(a, n)], vt_ref.shape[2],
        pl.program_id(2) >= n_ctx // tq, n_ctx, k_ref.shape[1] - n_ctx, tk)
    ot = acc[:HEAD_DIM] / acc[HEAD_DIM:HEAD_DIM + 1]
    pairs = [jnp.concatenate([ot[:, 2 * p * tq:(2 * p + 1) * tq], ot[:, (2 * p + 1) * tq:(2 * p + 2) * tq]], axis=0).T
             for p in range(GQA_GROUP // 2)]
    o_ref[0] = jnp.concatenate(pairs, axis=-1).astype(o_ref.dtype)


def _gqa_attention(flag, qp, kn, vt, *, nct, tq, tk):
    bsz, t, _ = qp.shape
    gw = GQA_GROUP * LANES
    return pl.pallas_call(
        functools.partial(_gqa_kernel, nct=nct, tk=tk),
        grid_spec=pltpu.PrefetchScalarGridSpec(
            num_scalar_prefetch=1,
            grid=(bsz, GQA_KV_HEADS, t // tq),
            in_specs=[
                pl.BlockSpec((1, tq, gw), lambda b, g, i, f: (b, i, g)),
                pl.BlockSpec((1, t, LANES), lambda b, g, i, f: (b, 0, 0)),
                pl.BlockSpec((1, 1, vt.shape[2], t), lambda b, g, i, f: (b, g, 0, 0)),
            ],
            out_specs=pl.BlockSpec((1, tq, GQA_GROUP * HEAD_DIM), lambda b, g, i, f: (b, i, g)),
        ),
        out_shape=jax.ShapeDtypeStruct((bsz, t, GQA_HEADS * HEAD_DIM), _BF),
        compiler_params=_params(("arbitrary", "arbitrary", "arbitrary")),
        name="gqa_attention",
    )(flag, qp, kn, vt)


def _diff_kernel(flag_ref, q_ref, k_ref, vt_ref, lq1_ref, lk1_ref, lq2_ref, lk2_ref, g_ref, o_ref, *,
                 nct, tk, lam_init):
    tq = q_ref.shape[1]
    n_ctx = nct * TM
    acc = _attend_guarded(
        flag_ref[0] != 0, _query_slots_t(q_ref), lambda a, n: k_ref[0, pl.ds(a, n), :],
        lambda a, n: vt_ref[0, 0, :, pl.ds(a, n)], vt_ref.shape[2],
        pl.program_id(2) >= n_ctx // tq, n_ctx, k_ref.shape[1] - n_ctx, tk)
    lam = (jnp.exp(jnp.sum(lq1_ref[...] * lk1_ref[...], axis=-1, keepdims=True))
           - jnp.exp(jnp.sum(lq2_ref[...] * lk2_ref[...], axis=-1, keepdims=True)) + lam_init)
    a1 = acc[:, :tq]
    a2 = acc[:, tq:]
    ot = (a1[:DIFF_V_DIM] / a1[DIFF_V_DIM:DIFF_V_DIM + 1]
          - lam * (a2[:DIFF_V_DIM] / a2[DIFF_V_DIM:DIFF_V_DIM + 1]))
    ot = ot * lax.rsqrt(jnp.mean(ot * ot, axis=0, keepdims=True) + EPS) * g_ref[...]
    o_ref[0] = (ot * (1.0 - lam_init)).T.astype(o_ref.dtype)


def _diff_attention(flag, dqp, dkn, dvt, lq1, lk1, lq2, lk2, subln_g, *, nct, tq, tk, lam_init):
    bsz, t, _ = dqp.shape
    vec = pl.BlockSpec((1, HEAD_DIM), lambda b, n, i, f: (0, 0))
    return pl.pallas_call(
        functools.partial(_diff_kernel, nct=nct, tk=tk, lam_init=lam_init),
        grid_spec=pltpu.PrefetchScalarGridSpec(
            num_scalar_prefetch=1,
            grid=(bsz, DIFF_HEADS, t // tq),
            in_specs=[
                pl.BlockSpec((1, tq, 2 * LANES), lambda b, n, i, f: (b, i, n)),
                pl.BlockSpec((1, t, LANES), lambda b, n, i, f: (b, 0, n)),
                pl.BlockSpec((1, 1, dvt.shape[2], t), lambda b, n, i, f: (b, n, 0, 0)),
                vec, vec, vec, vec,
                pl.BlockSpec((DIFF_V_DIM, 1), lambda b, n, i, f: (0, 0)),
            ],
            out_specs=pl.BlockSpec((1, tq, DIFF_V_DIM), lambda b, n, i, f: (b, i, n)),
        ),
        out_shape=jax.ShapeDtypeStruct((bsz, t, DIFF_HEADS * DIFF_V_DIM), _BF),
        compiler_params=_params(("arbitrary", "arbitrary", "arbitrary")),
        name="diff_attention",
    )(flag, dqp, dkn, dvt, lq1, lk1, lq2, lk2, subln_g)


def _merge_mlp_kernel(*refs, n_x, nct, skip, final):
    x_refs = refs[:n_x]
    (mod_ref, g1_ref, wg_ref, bg_ref, yr_ref, ya_ref, yd_ref, wb_ref, wo_ref,
     g2_ref, wu_ref, wd_ref, gf_ref, o_ref) = refs[n_x:]
    x = _stream_tile(x_refs, pl.program_id(1) + skip, nct)
    d = x.shape[-1]
    mod = lambda n: mod_ref[0, :, n * d:(n + 1) * d]
    hn = _rms_mod(x, g1_ref[...], mod(1), mod(0)).astype(_BF)
    ys = (yr_ref[0], ya_ref[0], yd_ref[0])
    m = None
    for n in range(N_BRANCH):
        g = _sigmoid(_dot(hn, wg_ref[:, n * d:(n + 1) * d]) + bg_ref[:, n * d:(n + 1) * d])
        term = g * _dot(ys[n], wb_ref[n])
        m = term if m is None else m + term
    x = x + mod(2) * _dot(m.astype(_BF), wo_ref[...])

    h = _rms_mod(x, g2_ref[...], mod(4), mod(3)).astype(_BF)
    acc = None
    for c in range(wu_ref.shape[-1] // d):
        u = jnp.maximum(_dot(h, wu_ref[:, c * d:(c + 1) * d]), 0.0)
        part = _dot((u * u).astype(_BF), wd_ref[c * d:(c + 1) * d, :])
        acc = part if acc is None else acc + part
    y = x + mod(5) * acc
    if final:
        y = y * lax.rsqrt(jnp.mean(y * y, axis=-1, keepdims=True) + EPS) * gf_ref[...]
    o_ref[0] = y


def _merge_mlp(xs, mod3, g1, w_gate, b_gate, y_rec, y_gqa, y_diff, w_branch, w_out, g2, w_up, w_down, final_g,
               nct, skip, final):
    bsz, t, bw = y_rec.shape
    d = xs[0].shape[-1]
    d_ff = w_up.shape[-1]
    nt = t // TM - skip
    row = _mod_row(nct, skip)
    tok = lambda w: pl.BlockSpec((1, TM, w), lambda b, i: (b, i + skip, 0))
    vec = pl.BlockSpec((1, d), lambda b, i: (0, 0))
    return pl.pallas_call(
        functools.partial(_merge_mlp_kernel, n_x=len(xs), nct=nct, skip=skip, final=final),
        grid=(bsz, nt),
        in_specs=_stream_specs(xs, skip, nct) + [
            pl.BlockSpec((1, 1, 6 * d), lambda b, i: (row(b, i, bsz), 0, 0)),
            vec,
            _resident((d, N_BRANCH * d), lambda b, i: (0, 0)),
            pl.BlockSpec((1, N_BRANCH * d), lambda b, i: (0, 0)),
            tok(bw), tok(bw), tok(bw),
            _resident((N_BRANCH, bw, d), lambda b, i: (0, 0, 0)),
            _resident((d, d), lambda b, i: (0, 0)),
            vec,
            _resident((d, d_ff), lambda b, i: (0, 0)),
            _resident((d_ff, d), lambda b, i: (0, 0)),
            vec,
        ],
        out_specs=pl.BlockSpec((1, TM, d), lambda b, i: (b, i, 0)),
        out_shape=jax.ShapeDtypeStruct((bsz, nt * TM, d), _F32),
        compiler_params=_params(("arbitrary", "arbitrary")),
        name="merge_mlp",
    )(*xs, mod3, g1, w_gate, b_gate, y_rec, y_gqa, y_diff, w_branch, w_out, g2, w_up, w_down, final_g)


def _rope_tables(n_ctx, s):
    rows = s // GRID_W
    pos_r = jnp.repeat(jnp.arange(rows, dtype=_F32), GRID_W)
    pos_c = jnp.tile(jnp.arange(GRID_W, dtype=_F32), rows)
    n_freq = HEAD_DIM // 4
    inv = ROPE_THETA ** (-jnp.arange(n_freq, dtype=_F32) * 2.0 / (HEAD_DIM // 2))
    ang_r = pos_r[:, None] * inv
    ang_c = pos_c[:, None] * inv
    ang = jnp.concatenate([ang_r, ang_r, ang_c, ang_c], axis=-1)
    sign = jnp.tile(jnp.concatenate([-jnp.ones(n_freq, _F32), jnp.ones(n_freq, _F32)]), 2)
    cos = jnp.concatenate([jnp.ones((n_ctx, HEAD_DIM), _F32), jnp.cos(ang)], axis=0)
    sin = jnp.concatenate([jnp.zeros((n_ctx, HEAD_DIM), _F32), jnp.sin(ang) * sign], axis=0)
    return jnp.tile(cos, (1, LANES // HEAD_DIM)), jnp.tile(sin, (1, LANES // HEAD_DIM))


def _block_diag(w):
    nb, n, _ = w.shape
    eye = jnp.eye(nb, dtype=w.dtype)
    return jnp.einsum('ncd,nm->ncmd', w, eye).reshape(nb * n, nb * n)


def _forward(x, c, ctx, c_ctx, w_mod, b_mod, norm1_g, w_in, b_gate, conv_w, conv_b, w_rg, b_rg, w_ig, b_ig,
             lru_lambda, q_norm_g, k_norm_g, lambda_q1, lambda_k1, lambda_q2, lambda_k2, subln_g, w_branch,
             w_out, norm2_g, w_up, w_down, final_g, *, tq_gqa, tq_diff, tk):
    bsz, s, d = x.shape
    n_ctx = ctx.shape[1]
    depth = w_mod.shape[0]
    assert n_ctx % TM == 0 and n_ctx > 0 and s % TM == 0 and s % tk == 0 and bsz < MOD_ROWS
    assert TM % tq_gqa == 0 and TM % tq_diff == 0
    nct = n_ctx // TM
    t = n_ctx + s

    cin = jnp.zeros((MOD_ROWS, d), _F32).at[:bsz].set(c).at[bsz].set(c_ctx)
    mod = _modulation(cin, w_mod, b_mod)
    cos, sin = _rope_tables(n_ctx, s)
    bd = jnp.kron(jnp.eye(2 * LANES // HEAD_DIM, dtype=_F32), jnp.ones((HEAD_DIM, HEAD_DIM), _F32)).astype(_BF)
    xs = (ctx, x)

    for l in range(depth):
        last = l == depth - 1
        skip = nct if last else 0
        lam_init = 0.8 - 0.6 * math.exp(-0.3 * l)
        mod3 = mod[l].reshape(MOD_ROWS, 1, 6 * d)
        w_proj = w_in[l, :, :N_PROJ].astype(_BF)
        w_gate = w_in[l, :, N_PROJ:].astype(_BF)
        gq = jnp.tile(q_norm_g[l], LANES // HEAD_DIM).reshape(1, LANES)
        gk = jnp.tile(k_norm_g[l], LANES // HEAD_DIM).reshape(1, LANES)
        z0, gz1, qp, kn, vt, dqp, dkn, dvt, stats = _inproj(
            xs, mod3, norm1_g[l].reshape(1, d), w_proj, gq, gk, cos, sin, bd, nct, t)

        gates = lambda dr: (_block_diag(w_rg[l, dr]).astype(_BF), b_rg[l, dr].reshape(1, -1),
                            _block_diag(w_ig[l, dr]).astype(_BF), b_ig[l, dr].reshape(1, -1),
                            lru_lambda[l, dr].reshape(1, -1))
        hf, u_conv = _lru_sweep(z0, gates(0), nct, False, conv=(conv_w[l], conv_b[l].reshape(1, -1)))
        y_rec = _lru_sweep(u_conv, gates(1), nct, True, hf=hf, gz1=gz1)

        st = jnp.max(stats, axis=(0, 1, 3))
        gqa_bound = math.sqrt(HEAD_DIM) * jnp.max(jnp.abs(q_norm_g[l])) * jnp.max(jnp.abs(k_norm_g[l]))
        gqa_ok = jnp.logical_and(gqa_bound * BOUND_SLACK <= SAFE_LOGIT, st[2] <= SAFE_VALUE)
        diff_ok = jnp.logical_and(st[0] * st[1] * BOUND_SLACK <= SAFE_LOGIT * SAFE_LOGIT, st[3] <= SAFE_VALUE)
        y_gqa = _gqa_attention(gqa_ok.astype(jnp.int32).reshape(1), qp, kn, vt, nct=nct, tq=tq_gqa, tk=tk)
        vec = lambda a: a[l].reshape(1, HEAD_DIM)
        y_diff = _diff_attention(diff_ok.astype(jnp.int32).reshape(1), dqp, dkn, dvt, vec(lambda_q1), vec(lambda_k1),
                                 vec(lambda_q2), vec(lambda_k2), subln_g[l].reshape(DIFF_V_DIM, 1),
                                 nct=nct, tq=tq_diff, tk=tk, lam_init=lam_init)

        xs = (_merge_mlp(xs, mod3, norm1_g[l].reshape(1, d), w_gate, b_gate[l].reshape(1, -1), y_rec, y_gqa, y_diff,
                         w_branch[l].astype(_BF), w_out[l].astype(_BF), norm2_g[l].reshape(1, d),
                         w_up[l].astype(_BF), w_down[l].astype(_BF), final_g.reshape(1, d), nct, skip, last),)
    return xs[0]


def kernel(x, c, ctx, c_ctx, w_mod, b_mod, norm1_g, w_in, b_gate, conv_w, conv_b, w_rg, b_rg, w_ig, b_ig, lru_lambda, q_norm_g, k_norm_g, lambda_q1, lambda_k1, lambda_q2, lambda_k2, subln_g, w_branch, w_out, norm2_g, w_up, w_down, final_g):
    return _forward(x, c, ctx, c_ctx, w_mod, b_mod, norm1_g, w_in, b_gate, conv_w, conv_b, w_rg, b_rg, w_ig, b_ig,
                    lru_lambda, q_norm_g, k_norm_g, lambda_q1, lambda_k1, lambda_q2, lambda_k2, subln_g, w_branch,
                    w_out, norm2_g, w_up, w_down, final_g, tq_gqa=256, tq_diff=256, tk=2048)
```
